```python
import math
import jax, jax.numpy as jnp
from jax import lax
import numpy as np

D_MODEL = 1024
BATCH = 8
SEQ = 4096
DEPTH = 1

HEAD_DIM = 64
N_HEADS = D_MODEL // HEAD_DIM
N_HEADS_MOBA = N_HEADS // 2
N_HEADS_DIL = N_HEADS - N_HEADS_MOBA
MOBA_BLOCK = 256
MOBA_TOPK = 3
Q_BLOCK = 128
DILATED_BRANCHES = ((128, 1), (512, 4), (2048, 16))
D_FF = ((8 * D_MODEL // 3 + 255) // 256) * 256
ROPE_THETA = 10000.0
RMS_EPS = 1e-6
NEG_INF = -1e30
ATTN_SCALE = 1.0 / math.sqrt(HEAD_DIM)

kernel_name = "hybrid_moba_dilated_parallel_heads"


def rmsnorm(x, g):
    xf = x.astype(jnp.float32)
    y = xf * lax.rsqrt(jnp.mean(xf * xf, axis=-1, keepdims=True) + RMS_EPS)
    return (y * g.astype(jnp.float32)).astype(x.dtype)


def rope(t):
    S, D = t.shape[1], t.shape[3]
    inv_freq = ROPE_THETA ** (-jnp.arange(0, D, 2, dtype=jnp.float32) / D)
    ang = jnp.arange(S, dtype=jnp.float32)[:, None] * inv_freq[None, :]
    cos = jnp.concatenate([jnp.cos(ang), jnp.cos(ang)], -1)[None, :, None, :]
    sin = jnp.concatenate([jnp.sin(ang), jnp.sin(ang)], -1)[None, :, None, :]
    tf = t.astype(jnp.float32)
    t1, t2 = tf[..., : D // 2], tf[..., D // 2:]
    rot = jnp.concatenate([-t2, t1], -1)
    return (tf * cos + rot * sin).astype(t.dtype)


def moba_attention(q, k, v):
    B, S, H, D = q.shape
    s_pad = -(-S // MOBA_BLOCK) * MOBA_BLOCK
    pad = ((0, 0), (0, s_pad - S), (0, 0), (0, 0))
    q, k, v = (jnp.pad(t, pad) for t in (q, k, v))
    nb = s_pad // MOBA_BLOCK
    nc = s_pad // Q_BLOCK
    k_sel = min(MOBA_TOPK, nb)
    kb = k.reshape(B, nb, MOBA_BLOCK, H, D).transpose(0, 3, 1, 2, 4)
    vb = v.reshape(B, nb, MOBA_BLOCK, H, D).transpose(0, 3, 1, 2, 4)
    k_mean = jnp.mean(kb.astype(jnp.float32), axis=3)
    q_blocks = q.reshape(B, nc, Q_BLOCK, H, D).transpose(0, 1, 3, 2, 4)
    head_idx = jnp.arange(H)[:, None, None]
    q_local = jnp.arange(Q_BLOCK)
    k_local = jnp.arange(MOBA_BLOCK)
    blk_ids = jnp.arange(nb)

    def one_query_block(i):
        b = i // nc
        c = i % nc
        qc = q_blocks[b, c]
        kbh = kb[b]
        vbh = vb[b]
        own = (c * Q_BLOCK) // MOBA_BLOCK
        gate = jnp.einsum('hqd,hnd->hqn', qc.astype(jnp.float32), k_mean[b])
        gate = jnp.where(blk_ids[None, None, :] < own, gate, NEG_INF)
        _, sel = lax.top_k(gate, k_sel)
        valid = sel < own
        ks = kbh[head_idx, sel]
        vs = vbh[head_idx, sel]
        s_sel = jnp.einsum('hqd,hqkld->hqkl', qc, ks).astype(jnp.float32) * ATTN_SCALE
        s_sel = jnp.where(valid[..., None], s_sel, NEG_INF)
        s_sel = s_sel.reshape(H, Q_BLOCK, k_sel * MOBA_BLOCK)
        k_own = lax.dynamic_index_in_dim(kbh, own, axis=1, keepdims=False)
        v_own = lax.dynamic_index_in_dim(vbh, own, axis=1, keepdims=False)
        s_own = jnp.einsum('hqd,hld->hql', qc, k_own).astype(jnp.float32) * ATTN_SCALE
        q_pos = c * Q_BLOCK + q_local
        k_pos = own * MOBA_BLOCK + k_local
        s_own = jnp.where(k_pos[None, None, :] <= q_pos[None, :, None], s_own, NEG_INF)
        p = jax.nn.softmax(jnp.concatenate([s_sel, s_own], axis=-1), axis=-1).astype(v.dtype)
        p_sel = p[..., : k_sel * MOBA_BLOCK].reshape(H, Q_BLOCK, k_sel, MOBA_BLOCK)
        p_own = p[..., k_sel * MOBA_BLOCK:]
        return (jnp.einsum('hqkl,hqkld->qhd', p_sel, vs)
                + jnp.einsum('hql,hld->qhd', p_own, v_own))

    out = lax.map(one_query_block, jnp.arange(B * nc))
    return out.reshape(B, s_pad, H, D)[:, :S]


def dilated_branch(q, k, v, window, dilation):
    B, S, H, D = q.shape
    L = window // dilation
    s_pad = -(-S // window) * window
    n_sub = s_pad // dilation
    nb = n_sub // L
    pad = ((0, 0), (0, s_pad - S), (0, 0), (0, 0))

    def to_blocks(t):
        t = jnp.pad(t, pad).reshape(B, n_sub, dilation, H, D).transpose(0, 2, 1, 3, 4)
        return t.reshape(B, dilation, nb, L, H, D)

    def with_prev(t):
        prev = jnp.pad(t, ((0, 0), (0, 0), (1, 0), (0, 0), (0, 0), (0, 0)))[:, :, :-1]
        return jnp.concatenate([prev, t], axis=3)

    qb = to_blocks(q)
    kk = with_prev(to_blocks(k))
    vv = with_prev(to_blocks(v))
    logits = jnp.einsum('brnqhd,brnkhd->brnhqk', qb, kk).astype(jnp.float32) * ATTN_SCALE
    qi = jnp.arange(L)[:, None] + L
    kj = jnp.arange(2 * L)[None, :]
    rel = qi - kj
    band = (rel >= 0) & (rel <= L)
    exists = (jnp.arange(nb)[:, None, None] > 0) | (kj[None] >= L)
    mask = band[None] & exists
    logits = jnp.where(mask[None, None, :, None], logits, NEG_INF)
    lse = jax.nn.logsumexp(logits, axis=-1)
    p = jnp.exp(logits - lse[..., None]).astype(v.dtype)
    o = jnp.einsum('brnhqk,brnkhd->brnqhd', p, vv)
    o = o.reshape(B, dilation, n_sub, H, D).transpose(0, 2, 1, 3, 4).reshape(B, s_pad, H, D)
    lse = lse.transpose(0, 1, 2, 4, 3).reshape(B, dilation, n_sub, H)
    lse = lse.transpose(0, 2, 1, 3).reshape(B, s_pad, H)
    return o[:, :S], lse[:, :S]


def dilated_mixture(q, k, v):
    outs, lses = [], []
    for window, dilation in DILATED_BRANCHES:
        o, lse = dilated_branch(q, k, v, window, dilation)
        outs.append(o)
        lses.append(lse)
    alpha = jax.nn.softmax(jnp.stack(lses, axis=0), axis=0)
    out = jnp.sum(alpha[..., None] * jnp.stack(outs, axis=0).astype(jnp.float32), axis=0)
    return out.astype(q.dtype)


def setup_inputs(seed: int = 0) -> dict:
    key = jax.random.key(seed)
    ks = jax.random.split(key, 12)
    f32 = jnp.float32
    x = jax.random.normal(ks[0], (BATCH, SEQ, D_MODEL), f32)
    attn_norm = 1.0 + 0.02 * jax.random.normal(ks[1], (D_MODEL,), f32)
    w_in = jax.random.normal(ks[2], (D_MODEL, 3 * D_MODEL), f32) * D_MODEL ** -0.5
    moba_out_norm = 1.0 + 0.02 * jax.random.normal(ks[3], (N_HEADS_MOBA * HEAD_DIM,), f32)
    dil_out_norm = 1.0 + 0.02 * jax.random.normal(ks[4], (N_HEADS_DIL * HEAD_DIM,), f32)
    w_out = jax.random.normal(ks[5], (D_MODEL, D_MODEL), f32) * D_MODEL ** -0.5
    ffn_norm = 1.0 + 0.02 * jax.random.normal(ks[6], (D_MODEL,), f32)
    w_gate = jax.random.normal(ks[7], (D_MODEL, D_FF), f32) * D_MODEL ** -0.5
    w_up = jax.random.normal(ks[8], (D_MODEL, D_FF), f32) * D_MODEL ** -0.5
    w_down = jax.random.normal(ks[9], (D_FF, D_MODEL), f32) * D_FF ** -0.5
    final_norm = 1.0 + 0.02 * jax.random.normal(ks[10], (D_MODEL,), f32)
    return {"x": x, "attn_norm": attn_norm, "w_in": w_in,
            "moba_out_norm": moba_out_norm, "dil_out_norm": dil_out_norm,
            "w_out": w_out, "ffn_norm": ffn_norm, "w_gate": w_gate,
            "w_up": w_up, "w_down": w_down, "final_norm": final_norm}


def reference(x, attn_norm, w_in, moba_out_norm, dil_out_norm, w_out,
              ffn_norm, w_gate, w_up, w_down, final_norm):
    B, S, _ = x.shape
    for _layer in range(DEPTH):
        h = rmsnorm(x, attn_norm)
        qkv = jnp.einsum('bsd,de->bse', h, w_in).reshape(B, S, 3, N_HEADS, HEAD_DIM)
        q = rope(qkv[:, :, 0])
        k = rope(qkv[:, :, 1])
        v = qkv[:, :, 2]
        a = N_HEADS_MOBA
        o_moba = moba_attention(q[:, :, :a], k[:, :, :a], v[:, :, :a])
        o_dil = dilated_mixture(q[:, :, a:], k[:, :, a:], v[:, :, a:])
        o_moba = rmsnorm(o_moba.reshape(B, S, -1), moba_out_norm)
        o_dil = rmsnorm(o_dil.reshape(B, S, -1), dil_out_norm)
        mixed = jnp.concatenate([o_moba, o_dil], axis=-1)
        x = x + jnp.einsum('bse,ed->bsd', mixed, w_out)
        h = rmsnorm(x, ffn_norm)
        g = jnp.einsum('bsd,df->bsf', h, w_gate)
        u = jnp.einsum('bsd,df->bsf', h, w_up)
        x = x + jnp.einsum('bsf,fd->bsd', jax.nn.silu(g) * u, w_down)
    return rmsnorm(x, final_norm)
```

```python
import functools
import math

import jax
import jax.numpy as jnp
from jax import lax
from jax.experimental import pallas as pl
from jax.experimental.pallas import tpu as pltpu

F32 = jnp.float32
BF16 = jnp.bfloat16

D_MODEL = 1024
HEAD_DIM = 64
N_HEADS = D_MODEL // HEAD_DIM
N_HEADS_MOBA = N_HEADS // 2
N_HEADS_DIL = N_HEADS - N_HEADS_MOBA
MOBA_BLOCK = 256
MOBA_TOPK = 3
DILATIONS = (1, 4, 16)
WINDOW_LEN = 128
D_FF = 2816
ROPE_THETA = 10000.0
RMS_EPS = 1e-6
NEG_INF = -1e30
ATTN_SCALE = 1.0 / math.sqrt(HEAD_DIM)

LANES = 128
SUM_LANE = HEAD_DIM
MAX_LANE = HEAD_DIM + 1
ONEHOT_LANE0 = HEAD_DIM

TOKEN_TILE = 512
FFN_CHUNK = 1408
MERGE_ROWS = 256
VMEM_LIMIT_SMALL = 40 * 1024 * 1024
VMEM_LIMIT_FFN = 56 * 1024 * 1024

_NT = (((1,), (1,)), ((), ()))
_TN = (((0,), (0,)), ((), ()))


def _rms(y, gain):
    return y * lax.rsqrt(jnp.mean(y * y, axis=-1, keepdims=True) + RMS_EPS) * gain


def _qkv_kernel(x_ref, g_ref, w_ref, cos_ref, sin_ref,
                qm_ref, km_ref, vm_ref, qd_ref, kd_ref, vd_ref):
    tile = x_ref.shape[1]
    h = _rms(x_ref[0], g_ref[...]).astype(BF16)
    cos = cos_ref[...]
    sin = sin_ref[...]
    lane = lax.broadcasted_iota(jnp.int32, (tile, LANES), 1)
    row = lax.broadcasted_iota(jnp.int32, (tile, LANES), 0)
    low_head = lane < HEAD_DIM
    first_half = (lane & (HEAD_DIM - 1)) < HEAD_DIM // 2
    block_of_row = (pl.program_id(1) * tile + row) // MOBA_BLOCK
    onehot = jnp.where(lane - ONEHOT_LANE0 == block_of_row, 1.0, 0.0)
    ones_col = jnp.where(lane == SUM_LANE, 1.0, 0.0)
    zeros = jnp.zeros((tile, LANES), F32)

    def rope(y):
        rot = jnp.where(first_half, pltpu.roll(y, LANES - HEAD_DIM // 2, 1),
                        pltpu.roll(y, HEAD_DIM // 2, 1))
        return y * cos + rot * sin

    def split(y, pad):
        return (jnp.where(low_head, y, pad),
                jnp.where(low_head, pltpu.roll(y, HEAD_DIM, 1), pad))

    q = jnp.dot(h, w_ref[:, 0:D_MODEL], preferred_element_type=F32)
    k = jnp.dot(h, w_ref[:, D_MODEL:2 * D_MODEL], preferred_element_type=F32)
    v = jnp.dot(h, w_ref[:, 2 * D_MODEL:3 * D_MODEL], preferred_element_type=F32)
    for pair in range(N_HEADS // 2):
        cols = slice(pair * LANES, (pair + 1) * LANES)
        moba = pair < N_HEADS_MOBA // 2
        q_ref, k_ref, v_ref = (qm_ref, km_ref, vm_ref) if moba else (qd_ref, kd_ref, vd_ref)
        head0 = 2 * pair - (0 if moba else N_HEADS_MOBA)
        qs = split(rope(q[:, cols]) * ATTN_SCALE, zeros)
        ks = split(rope(k[:, cols]), onehot if moba else zeros)
        vs = split(v[:, cols], ones_col)
        for a in range(2):
            q_ref[0, head0 + a] = qs[a].astype(BF16)
            k_ref[0, head0 + a] = ks[a].astype(BF16)
            v_ref[0, head0 + a] = vs[a].astype(BF16)


def _qkv_call(x, attn_norm, w_in, cos, sin):
    B, S, D = x.shape
    tile = TOKEN_TILE
    head_shape = jax.ShapeDtypeStruct((B, N_HEADS_MOBA, S, LANES), BF16)
    head_spec = pl.BlockSpec((1, N_HEADS_MOBA, tile, LANES), lambda b, s: (b, 0, s, 0))
    return pl.pallas_call(
        _qkv_kernel,
        grid=(B, S // tile),
        in_specs=[
            pl.BlockSpec((1, tile, D), lambda b, s: (b, s, 0)),
            pl.BlockSpec((1, D), lambda b, s: (0, 0)),
            pl.BlockSpec((D, 3 * D), lambda b, s: (0, 0), pipeline_mode=pl.Buffered(1)),
            pl.BlockSpec((tile, LANES), lambda b, s: (s, 0)),
            pl.BlockSpec((tile, LANES), lambda b, s: (s, 0)),
        ],
        out_specs=[head_spec] * 6,
        out_shape=[head_shape] * 6,
        compiler_params=pltpu.CompilerParams(
            dimension_semantics=("arbitrary", "arbitrary"),
            vmem_limit_bytes=VMEM_LIMIT_SMALL),
        name="qkv_rope",
    )(x, attn_norm.reshape(1, D), w_in, cos, sin)


def _moba_kernel(q_ref, k_ref, v_ref, o_ref, kmean_ref):
    blk = MOBA_BLOCK
    n_blocks = k_ref.shape[2] // blk
    own = pl.program_id(2)

    @pl.when(own == 0)
    def _():
        for n in range(n_blocks):
            kb = k_ref[0, 0, n * blk:(n + 1) * blk, :].astype(F32)
            kmean_ref[n:n + 1, :] = jnp.mean(kb, axis=0, keepdims=True)

    q = q_ref[0, 0]

    gate_t = lax.dot_general(kmean_ref[...], q.astype(F32), _NT,
                             precision=lax.Precision.HIGHEST,
                             preferred_element_type=F32)
    n_idx = lax.broadcasted_iota(jnp.int32, (n_blocks, blk), 0)
    past = n_idx < own
    gate_t = jnp.where(past, gate_t, NEG_INF)
    rank = jnp.zeros((n_blocks, blk), jnp.int32)
    for n in range(n_blocks):
        other = gate_t[n:n + 1, :]
        ahead = (other > gate_t) | ((other == gate_t) & (n < n_idx))
        rank = rank + ahead.astype(jnp.int32)
    selected = (rank < MOBA_TOPK) & past
    bias_t = jnp.where(selected, 0.0, NEG_INF).astype(BF16)
    sub = lax.broadcasted_iota(jnp.int32, (n_blocks, LANES), 0)
    lane = lax.broadcasted_iota(jnp.int32, (n_blocks, LANES), 1)
    place = jnp.where(lane == sub + ONEHOT_LANE0, 1.0, 0.0).astype(BF16)
    bias = lax.dot_general(bias_t, place, _TN, preferred_element_type=F32)
    q_biased = (q.astype(F32) + bias).astype(BF16)

    def kv(j):
        rows = pl.ds(pl.multiple_of(j * blk, blk), blk)
        return k_ref[0, 0, rows, :], v_ref[0, 0, rows, :]

    k_own, v_own = kv(own)
    s = lax.dot_general(q, k_own, _NT, preferred_element_type=F32)
    r_idx = lax.broadcasted_iota(jnp.int32, (blk, blk), 0)
    c_idx = lax.broadcasted_iota(jnp.int32, (blk, blk), 1)
    s = jnp.where(c_idx <= r_idx, s, NEG_INF)
    m = jnp.max(s, axis=1, keepdims=True)
    acc = jnp.dot(jnp.exp(s - m).astype(BF16), v_own, preferred_element_type=F32)

    def past_block(j, carry):
        m, acc = carry
        k_j, v_j = kv(j)
        s = lax.dot_general(q_biased, k_j, _NT, preferred_element_type=F32)
        m_new = jnp.maximum(m, jnp.max(s, axis=1, keepdims=True))
        p = jnp.exp(s - m_new).astype(BF16)
        acc = jnp.exp(m - m_new) * acc + jnp.dot(p, v_j, preferred_element_type=F32)
        return m_new, acc

    m, acc = lax.fori_loop(0, own, past_block, (m, acc))
    o_ref[0, 0] = (acc / acc[:, SUM_LANE:SUM_LANE + 1]).astype(BF16)


def _moba_call(q, k, v):
    B, H, S, _ = q.shape
    blk = MOBA_BLOCK
    seq_spec = pl.BlockSpec((1, 1, S, LANES), lambda b, h, i: (b, h, 0, 0))
    tile_spec = pl.BlockSpec((1, 1, blk, LANES), lambda b, h, i: (b, h, i, 0))
    return pl.pallas_call(
        _moba_kernel,
        grid=(B, H, S // blk),
        in_specs=[tile_spec, seq_spec, seq_spec],
        out_specs=tile_spec,
        out_shape=jax.ShapeDtypeStruct((B, H, S, LANES), BF16),
        scratch_shapes=[pltpu.VMEM((S // blk, LANES), F32)],
        compiler_params=pltpu.CompilerParams(
            dimension_semantics=("arbitrary", "arbitrary", "arbitrary"),
            vmem_limit_bytes=VMEM_LIMIT_SMALL),
        name="moba_attention",
    )(q, k, v)


def _dilated_kernel(q_ref, k_ref, v_ref, o_ref, stage_ref,
                    q4_ref, k4_ref, v4_ref, q16_ref, k16_ref, v16_ref,
                    acc1_ref, acc4_ref, acc16_ref):
    S = q_ref.shape[2]
    L = WINDOW_LEN
    q1_ref, k1_ref, v1_ref = q_ref.at[0, 0], k_ref.at[0, 0], v_ref.at[0, 0]

    for src, dst4, dst16 in ((q1_ref, q4_ref, q16_ref), (k1_ref, k4_ref, k16_ref),
                             (v1_ref, v4_ref, v16_ref)):
        stage_ref[...] = src[...].astype(F32)
        for d, dst in ((4, dst4), (16, dst16)):
            n = S // d
            for r in range(d):
                dst[r * n:(r + 1) * n, :] = stage_ref[pl.ds(r, n, stride=d), :].astype(BF16)

    r_idx = lax.broadcasted_iota(jnp.int32, (L, L), 0)
    c_idx = lax.broadcasted_iota(jnp.int32, (L, L), 1)
    lane = lax.broadcasted_iota(jnp.int32, (L, LANES), 1)

    def branch(d, qd_ref, kd_ref, vd_ref, acc_ref):
        tiles_per_residue = S // d // L

        def tile_step(idx, carry):
            base = pl.multiple_of(idx * L, L)
            prev = pl.multiple_of(jnp.maximum(idx - 1, 0) * L, L)
            t_in_residue = idx % tiles_per_residue
            residue = idx // tiles_per_residue
            has_prev = t_in_residue > 0
            q = qd_ref[pl.ds(base, L), :]
            s_cur = lax.dot_general(q, kd_ref[pl.ds(base, L), :], _NT,
                                    preferred_element_type=F32)
            s_cur = jnp.where(c_idx <= r_idx, s_cur, NEG_INF)
            s_prev = lax.dot_general(q, kd_ref[pl.ds(prev, L), :], _NT,
                                     preferred_element_type=F32)
            s_prev = jnp.where((c_idx >= r_idx) & has_prev, s_prev, NEG_INF)
            m = jnp.maximum(jnp.max(s_cur, axis=1, keepdims=True),
                            jnp.max(s_prev, axis=1, keepdims=True))
            acc = (jnp.dot(jnp.exp(s_cur - m).astype(BF16), vd_ref[pl.ds(base, L), :],
                           preferred_element_type=F32)
                   + jnp.dot(jnp.exp(s_prev - m).astype(BF16), vd_ref[pl.ds(prev, L), :],
                             preferred_element_type=F32))
            acc = jnp.where(lane == MAX_LANE, m, acc)
            start = t_in_residue * (L * d) + residue
            if d == 1:
                acc_ref[pl.ds(pl.multiple_of(start, L), L), :] = acc
            else:
                acc_ref[pl.ds(start, L, stride=d), :] = acc
            return carry

        lax.fori_loop(0, S // L, tile_step, 0)

    branch(1, q1_ref, k1_ref, v1_ref, acc1_ref)
    branch(4, q4_ref, k4_ref, v4_ref, acc4_ref)
    branch(16, q16_ref, k16_ref, v16_ref, acc16_ref)

    def merge(c, carry):
        rows = pl.ds(pl.multiple_of(c * MERGE_ROWS, MERGE_ROWS), MERGE_ROWS)
        parts = [ref[rows, :] for ref in (acc1_ref, acc4_ref, acc16_ref)]
        maxes = [p[:, MAX_LANE:MAX_LANE + 1] for p in parts]
        m = jnp.maximum(jnp.maximum(maxes[0], maxes[1]), maxes[2])
        total = sum(jnp.exp(mi - m) * p for mi, p in zip(maxes, parts))
        o_ref[0, 0, rows, :] = (total / total[:, SUM_LANE:SUM_LANE + 1]).astype(BF16)
        return carry

    lax.fori_loop(0, S // MERGE_ROWS, merge, 0)


def _dilated_call(q, k, v):
    B, H, S, _ = q.shape
    seq_spec = pl.BlockSpec((1, 1, S, LANES), lambda b, h: (b, h, 0, 0))
    return pl.pallas_call(
        _dilated_kernel,
        grid=(B, H),
        in_specs=[seq_spec] * 3,
        out_specs=seq_spec,
        out_shape=jax.ShapeDtypeStruct((B, H, S, LANES), BF16),
        scratch_shapes=([pltpu.VMEM((S, LANES), F32)]
                        + [pltpu.VMEM((S, LANES), BF16)] * 6
                        + [pltpu.VMEM((S, LANES), F32)] * 3),
        compiler_params=pltpu.CompilerParams(
            dimension_semantics=("arbitrary", "arbitrary"),
            vmem_limit_bytes=VMEM_LIMIT_SMALL),
        name="dilated_attention",
    )(q, k, v)


def _ffn_kernel(x_ref, om_ref, od_ref, gm_ref, gd_ref, wo_ref, gf_ref,
                wg_ref, wu_ref, wd_ref, gl_ref, out_ref):
    tile = x_ref.shape[1]
    lane = lax.broadcasted_iota(jnp.int32, (tile, LANES), 1)
    low_head = lane < HEAD_DIM

    def heads(ref):
        pairs = []
        for p in range(ref.shape[1] // 2):
            a = ref[0, 2 * p].astype(F32)
            b = ref[0, 2 * p + 1].astype(F32)
            pairs.append(jnp.where(low_head, a, pltpu.roll(b, HEAD_DIM, 1)))
        return jnp.concatenate(pairs, axis=1)

    mixed = jnp.concatenate([_rms(heads(om_ref), gm_ref[...]),
                             _rms(heads(od_ref), gd_ref[...])], axis=1).astype(BF16)
    x1 = x_ref[0] + jnp.dot(mixed, wo_ref[...], preferred_element_type=F32)
    h = _rms(x1, gf_ref[...]).astype(BF16)
    x2 = x1
    for c in range(D_FF // FFN_CHUNK):
        cols = slice(c * FFN_CHUNK, (c + 1) * FFN_CHUNK)
        g = jnp.dot(h, wg_ref[:, cols], preferred_element_type=F32)
        u = jnp.dot(h, wu_ref[:, cols], preferred_element_type=F32)
        act = (g * jax.nn.sigmoid(g) * u).astype(BF16)
        x2 = x2 + jnp.dot(act, wd_ref[cols, :], preferred_element_type=F32)
    out_ref[0] = _rms(x2, gl_ref[...])


def _ffn_call(x, o_moba, o_dil, moba_out_norm, dil_out_norm, w_out, ffn_norm,
              w_gate, w_up, w_down, final_norm):
    B, S, D = x.shape
    tile = TOKEN_TILE
    half = N_HEADS_MOBA * HEAD_DIM

    def const(shape):
        return pl.BlockSpec(shape, lambda b, s: (0,) * len(shape), pipeline_mode=pl.Buffered(1))

    row_spec = pl.BlockSpec((1, tile, D), lambda b, s: (b, s, 0))
    head_spec = pl.BlockSpec((1, N_HEADS_MOBA, tile, LANES), lambda b, s: (b, 0, s, 0))
    return pl.pallas_call(
        _ffn_kernel,
        grid=(B, S // tile),
        in_specs=[row_spec, head_spec, head_spec,
                  const((1, half)), const((1, half)), const((D, D)), const((1, D)),
                  const((D, D_FF)), const((D, D_FF)), const((D_FF, D)), const((1, D))],
        out_specs=row_spec,
        out_shape=jax.ShapeDtypeStruct((B, S, D), F32),
        compiler_params=pltpu.CompilerParams(
            dimension_semantics=("arbitrary", "arbitrary"),
            vmem_limit_bytes=VMEM_LIMIT_FFN),
        name="outproj_ffn",
    )(x, o_moba, o_dil, moba_out_norm.reshape(1, half), dil_out_norm.reshape(1, half),
      w_out, ffn_norm.reshape(1, D), w_gate, w_up, w_down, final_norm.reshape(1, D))


def _rope_tables(S):
    inv_freq = ROPE_THETA ** (-jnp.arange(0, HEAD_DIM, 2, dtype=F32) / HEAD_DIM)
    ang = jnp.arange(S, dtype=F32)[:, None] * inv_freq[None, :]
    cos = jnp.concatenate([jnp.cos(ang)] * 4, axis=-1)
    sin = jnp.concatenate([-jnp.sin(ang), jnp.sin(ang)] * 2, axis=-1)
    return cos, sin


def kernel(x, attn_norm, w_in, moba_out_norm, dil_out_norm, w_out, ffn_norm, w_gate, w_up,
           w_down, final_norm):
    B, S, D = x.shape
    assert D == D_MODEL and S % TOKEN_TILE == 0 and S % (16 * WINDOW_LEN) == 0
    assert S // MOBA_BLOCK <= LANES - ONEHOT_LANE0
    cos, sin = _rope_tables(S)
    qm, km, vm, qd, kd, vd = _qkv_call(x, attn_norm, w_in.astype(BF16), cos, sin)
    o_moba = _moba_call(qm, km, vm)
    o_dil = _dilated_call(qd, kd, vd)
    return _ffn_call(x, o_moba, o_dil, moba_out_norm, dil_out_norm, w_out.astype(BF16),
                     ffn_norm, w_gate.astype(BF16), w_up.astype(BF16), w_down.astype(BF16),
                     final_norm)
```

```python
import functools
import math

import jax
import jax.numpy as jnp
from jax import lax
from jax.experimental import pallas as pl
from jax.experimental.pallas import tpu as pltpu

F32 = jnp.float32
BF16 = jnp.bfloat16

D_MODEL = 1024
HEAD_DIM = 64
N_HEADS = D_MODEL // HEAD_DIM
N_HEADS_MOBA = N_HEADS // 2
N_HEADS_DIL = N_HEADS - N_HEADS_MOBA
MOBA_BLOCK = 256
MOBA_TOPK = 3
DILATIONS = (1, 4, 16)
WINDOW_LEN = 128
D_FF = 2816
ROPE_THETA = 10000.0
RMS_EPS = 1e-6
NEG_INF = -1e30
ATTN_SCALE = 1.0 / math.sqrt(HEAD_DIM)

LANES = 128
SUM_LANE = HEAD_DIM
MAX_LANE = HEAD_DIM + 1
ONEHOT_LANE0 = HEAD_DIM

TOKEN_TILE = 512
FFN_CHUNK = 1408
MERGE_ROWS = 256
MOBA_UNROLL = 2
DILATED_UNROLL = 4
VMEM_LIMIT_SMALL = 40 * 1024 * 1024
VMEM_LIMIT_FFN = 56 * 1024 * 1024

_NT = (((1,), (1,)), ((), ()))
_TN = (((0,), (0,)), ((), ()))


def _rms(y, gain):
    return y * lax.rsqrt(jnp.mean(y * y, axis=-1, keepdims=True) + RMS_EPS) * gain


def _qkv_kernel(x_ref, g_ref, w_ref, wvt_ref, cos_ref, sin_ref,
                qm_ref, km_ref, vmt_ref, qd_ref, kd_ref, vd_ref):
    tile = x_ref.shape[1]
    moba_width = N_HEADS_MOBA * HEAD_DIM
    h = _rms(x_ref[0], g_ref[...]).astype(BF16)
    cos = cos_ref[...]
    sin = sin_ref[...]
    lane = lax.broadcasted_iota(jnp.int32, (tile, LANES), 1)
    row = lax.broadcasted_iota(jnp.int32, (tile, LANES), 0)
    low_head = lane < HEAD_DIM
    first_half = (lane & (HEAD_DIM - 1)) < HEAD_DIM // 2
    block_of_row = (pl.program_id(1) * tile + row) // MOBA_BLOCK
    onehot = jnp.where(lane - ONEHOT_LANE0 == block_of_row, 1.0, 0.0)
    ones_col = jnp.where(lane == SUM_LANE, 1.0, 0.0)
    zeros = jnp.zeros((tile, LANES), F32)

    def rope(y):
        rot = jnp.where(first_half, pltpu.roll(y, LANES - HEAD_DIM // 2, 1),
                        pltpu.roll(y, HEAD_DIM // 2, 1))
        return y * cos + rot * sin

    def split(y, pad):
        return (jnp.where(low_head, y, pad),
                jnp.where(low_head, pltpu.roll(y, HEAD_DIM, 1), pad))

    q = jnp.dot(h, w_ref[:, 0:D_MODEL], preferred_element_type=F32)
    k = jnp.dot(h, w_ref[:, D_MODEL:2 * D_MODEL], preferred_element_type=F32)
    v = jnp.dot(h, w_ref[:, 2 * D_MODEL + moba_width:3 * D_MODEL], preferred_element_type=F32)
    for pair in range(N_HEADS // 2):
        cols = slice(pair * LANES, (pair + 1) * LANES)
        moba = pair < N_HEADS_MOBA // 2
        q_ref, k_ref = (qm_ref, km_ref) if moba else (qd_ref, kd_ref)
        head0 = 2 * pair - (0 if moba else N_HEADS_MOBA)
        qs = split(rope(q[:, cols]) * ATTN_SCALE, zeros)
        ks = split(rope(k[:, cols]), onehot if moba else zeros)
        for a in range(2):
            q_ref[0, head0 + a] = qs[a].astype(BF16)
            k_ref[0, head0 + a] = ks[a].astype(BF16)
        if not moba:
            vs = split(v[:, head0 * HEAD_DIM:head0 * HEAD_DIM + LANES], ones_col)
            for a in range(2):
                vd_ref[0, head0 + a] = vs[a].astype(BF16)

    vt = lax.dot_general(wvt_ref[...], h, _NT, preferred_element_type=F32)
    sub = lax.broadcasted_iota(jnp.int32, (LANES - HEAD_DIM, tile), 0)
    ones_row = jnp.where(sub == SUM_LANE - HEAD_DIM, 1.0, 0.0)
    for head in range(N_HEADS_MOBA):
        slab = jnp.concatenate([vt[head * HEAD_DIM:(head + 1) * HEAD_DIM, :], ones_row],
                               axis=0).astype(BF16)
        for c in range(tile // MOBA_BLOCK):
            vmt_ref[0, head, c] = slab[:, c * MOBA_BLOCK:(c + 1) * MOBA_BLOCK]


def _qkv_call(x, attn_norm, w_in, wv_moba_t, cos, sin):
    B, S, D = x.shape
    tile = TOKEN_TILE
    H = N_HEADS_MOBA
    head_shape = jax.ShapeDtypeStruct((B, H, S, LANES), BF16)
    head_spec = pl.BlockSpec((1, H, tile, LANES), lambda b, s: (b, 0, s, 0))
    vt_shape = jax.ShapeDtypeStruct((B, H, S // MOBA_BLOCK, LANES, MOBA_BLOCK), BF16)
    vt_spec = pl.BlockSpec((1, H, tile // MOBA_BLOCK, LANES, MOBA_BLOCK),
                           lambda b, s: (b, 0, s, 0, 0))
    return pl.pallas_call(
        _qkv_kernel,
        grid=(B, S // tile),
        in_specs=[
            pl.BlockSpec((1, tile, D), lambda b, s: (b, s, 0)),
            pl.BlockSpec((1, D), lambda b, s: (0, 0)),
            pl.BlockSpec((D, 3 * D), lambda b, s: (0, 0), pipeline_mode=pl.Buffered(1)),
            pl.BlockSpec(wv_moba_t.shape, lambda b, s: (0, 0), pipeline_mode=pl.Buffered(1)),
            pl.BlockSpec((tile, LANES), lambda b, s: (s, 0)),
            pl.BlockSpec((tile, LANES), lambda b, s: (s, 0)),
        ],
        out_specs=[head_spec, head_spec, vt_spec, head_spec, head_spec, head_spec],
        out_shape=[head_shape, head_shape, vt_shape, head_shape, head_shape, head_shape],
        compiler_params=pltpu.CompilerParams(
            dimension_semantics=("arbitrary", "arbitrary"),
            vmem_limit_bytes=VMEM_LIMIT_SMALL),
        name="qkv_rope",
    )(x, attn_norm.reshape(1, D), w_in, wv_moba_t, cos, sin)


def _moba_kernel(q_ref, k_ref, vt_ref, o_ref, kmean_ref, s_ref):
    blk = MOBA_BLOCK
    n_blocks = k_ref.shape[2] // blk
    own = pl.program_id(2)

    @pl.when(own == 0)
    def _():
        for n in range(n_blocks):
            kb = k_ref[0, 0, n * blk:(n + 1) * blk, :].astype(F32)
            kmean_ref[n:n + 1, :] = jnp.mean(kb, axis=0, keepdims=True)

    q = q_ref[0, 0]

    gate_t = lax.dot_general(kmean_ref[...], q.astype(F32), _NT,
                             precision=lax.Precision.HIGHEST,
                             preferred_element_type=F32)
    n_idx = lax.broadcasted_iota(jnp.int32, (n_blocks, blk), 0)
    past = n_idx < own
    gate_t = jnp.where(past, gate_t, NEG_INF)
    rank = jnp.zeros((n_blocks, blk), jnp.int32)
    for n in range(n_blocks):
        other = gate_t[n:n + 1, :]
        ahead = (other > gate_t) | ((other == gate_t) & (n < n_idx))
        rank = rank + ahead.astype(jnp.int32)
    selected = (rank < MOBA_TOPK) & past
    bias_t = jnp.where(selected, 0.0, NEG_INF).astype(BF16)
    sub = lax.broadcasted_iota(jnp.int32, (n_blocks, LANES), 0)
    lane = lax.broadcasted_iota(jnp.int32, (n_blocks, LANES), 1)
    place = jnp.where(lane == sub + ONEHOT_LANE0, 1.0, 0.0).astype(BF16)
    bias = lax.dot_general(bias_t, place, _TN, preferred_element_type=F32)
    q_biased = (q.astype(F32) + bias).astype(BF16)

    def scores_t(j, queries):
        rows = pl.ds(pl.multiple_of(j * blk, blk), blk)
        return lax.dot_general(k_ref[0, 0, rows, :], queries, _NT, preferred_element_type=F32)

    def col_max(s):
        return jnp.max(s.reshape(blk // 8, 8, blk), axis=0)

    n_trips = (own + MOBA_UNROLL - 1) // MOBA_UNROLL

    def pass_a(t, m8):
        for u in range(MOBA_UNROLL):
            j = t * MOBA_UNROLL + u
            s = scores_t(j, q_biased)
            s_ref[pl.ds(pl.multiple_of(j * blk, blk), blk), :] = s
            m8 = jnp.maximum(m8, col_max(s))
        return m8

    m8 = lax.fori_loop(0, n_trips, pass_a, jnp.full((8, blk), NEG_INF, F32))

    s_own = scores_t(own, q)
    key_idx = lax.broadcasted_iota(jnp.int32, (blk, blk), 0)
    qry_idx = lax.broadcasted_iota(jnp.int32, (blk, blk), 1)
    s_own = jnp.where(key_idx <= qry_idx, s_own, NEG_INF)
    m = jnp.max(jnp.maximum(m8, col_max(s_own)), axis=0, keepdims=True)

    acc_t = jnp.dot(vt_ref[0, 0, own], jnp.exp(s_own - m).astype(BF16),
                    preferred_element_type=F32)

    def pass_b(t, acc_t):
        for u in range(MOBA_UNROLL):
            j = t * MOBA_UNROLL + u
            s = s_ref[pl.ds(pl.multiple_of(j * blk, blk), blk), :]
            acc_t = acc_t + jnp.dot(vt_ref[0, 0, j], jnp.exp(s - m).astype(BF16),
                                    preferred_element_type=F32)
        return acc_t

    acc_t = lax.fori_loop(0, n_trips, pass_b, acc_t)
    out_t = acc_t / acc_t[SUM_LANE:SUM_LANE + 1, :]
    o_ref[0, 0] = out_t.T.astype(BF16)


def _moba_call(q, k, vt):
    B, H, S, _ = q.shape
    blk = MOBA_BLOCK
    assert (S // blk) % MOBA_UNROLL == 0
    k_spec = pl.BlockSpec((1, 1, S, LANES), lambda b, h, i: (b, h, 0, 0))
    vt_spec = pl.BlockSpec((1, 1, S // blk, LANES, blk), lambda b, h, i: (b, h, 0, 0, 0))
    tile_spec = pl.BlockSpec((1, 1, blk, LANES), lambda b, h, i: (b, h, i, 0))
    return pl.pallas_call(
        _moba_kernel,
        grid=(B, H, S // blk),
        in_specs=[tile_spec, k_spec, vt_spec],
        out_specs=tile_spec,
        out_shape=jax.ShapeDtypeStruct((B, H, S, LANES), BF16),
        scratch_shapes=[pltpu.VMEM((S // blk, LANES), F32),
                        pltpu.VMEM((S, blk), F32)],
        compiler_params=pltpu.CompilerParams(
            dimension_semantics=("arbitrary", "arbitrary", "arbitrary"),
            vmem_limit_bytes=VMEM_LIMIT_SMALL),
        name="moba_attention",
    )(q, k, vt)


def _dilated_kernel(q_ref, k_ref, v_ref, o_ref, stage_ref,
                    q4_ref, k4_ref, v4_ref, q16_ref, k16_ref, v16_ref,
                    acc1_ref, acc4_ref, acc16_ref):
    S = q_ref.shape[2]
    L = WINDOW_LEN
    q1_ref, k1_ref, v1_ref = q_ref.at[0, 0], k_ref.at[0, 0], v_ref.at[0, 0]

    for src, dst4, dst16 in ((q1_ref, q4_ref, q16_ref), (k1_ref, k4_ref, k16_ref),
                             (v1_ref, v4_ref, v16_ref)):
        stage_ref[...] = src[...].astype(F32)
        for d, dst in ((4, dst4), (16, dst16)):
            n = S // d
            for r in range(d):
                dst[r * n:(r + 1) * n, :] = stage_ref[pl.ds(r, n, stride=d), :].astype(BF16)

    r_idx = lax.broadcasted_iota(jnp.int32, (L, L), 0)
    c_idx = lax.broadcasted_iota(jnp.int32, (L, L), 1)
    lane = lax.broadcasted_iota(jnp.int32, (L, LANES), 1)

    def branch(d, qd_ref, kd_ref, vd_ref, acc_ref):
        tiles_per_residue = S // d // L

        def trip(t, carry):
            for u in range(DILATED_UNROLL):
                tile_step(t * DILATED_UNROLL + u)
            return carry

        def tile_step(idx):
            base = pl.multiple_of(idx * L, L)
            prev = pl.multiple_of(jnp.maximum(idx - 1, 0) * L, L)
            t_in_residue = idx % tiles_per_residue
            residue = idx // tiles_per_residue
            has_prev = t_in_residue > 0
            q = qd_ref[pl.ds(base, L), :]
            s_cur = lax.dot_general(q, kd_ref[pl.ds(base, L), :], _NT,
                                    preferred_element_type=F32)
            s_cur = jnp.where(c_idx <= r_idx, s_cur, NEG_INF)
            s_prev = lax.dot_general(q, kd_ref[pl.ds(prev, L), :], _NT,
                                     preferred_element_type=F32)
            s_prev = jnp.where((c_idx >= r_idx) & has_prev, s_prev, NEG_INF)
            m = jnp.maximum(jnp.max(s_cur, axis=1, keepdims=True),
                            jnp.max(s_prev, axis=1, keepdims=True))
            acc = (jnp.dot(jnp.exp(s_cur - m).astype(BF16), vd_ref[pl.ds(base, L), :],
                           preferred_element_type=F32)
                   + jnp.dot(jnp.exp(s_prev - m).astype(BF16), vd_ref[pl.ds(prev, L), :],
                             preferred_element_type=F32))
            acc = jnp.where(lane == MAX_LANE, m, acc)
            start = t_in_residue * (L * d) + residue
            if d == 1:
                acc_ref[pl.ds(pl.multiple_of(start, L), L), :] = acc
            else:
                acc_ref[pl.ds(start, L, stride=d), :] = acc

        lax.fori_loop(0, S // L // DILATED_UNROLL, trip, 0)

    branch(1, q1_ref, k1_ref, v1_ref, acc1_ref)
    branch(4, q4_ref, k4_ref, v4_ref, acc4_ref)
    branch(16, q16_ref, k16_ref, v16_ref, acc16_ref)

    def merge(c, carry):
        rows = pl.ds(pl.multiple_of(c * MERGE_ROWS, MERGE_ROWS), MERGE_ROWS)
        parts = [ref[rows, :] for ref in (acc1_ref, acc4_ref, acc16_ref)]
        maxes = [p[:, MAX_LANE:MAX_LANE + 1] for p in parts]
        m = jnp.maximum(jnp.maximum(maxes[0], maxes[1]), maxes[2])
        total = sum(jnp.exp(mi - m) * p for mi, p in zip(maxes, parts))
        o_ref[0, 0, rows, :] = (total / total[:, SUM_LANE:SUM_LANE + 1]).astype(BF16)
        return carry

    lax.fori_loop(0, S // MERGE_ROWS, merge, 0)


def _dilated_call(q, k, v):
    B, H, S, _ = q.shape
    seq_spec = pl.BlockSpec((1, 1, S, LANES), lambda b, h: (b, h, 0, 0))
    return pl.pallas_call(
        _dilated_kernel,
        grid=(B, H),
        in_specs=[seq_spec] * 3,
        out_specs=seq_spec,
        out_shape=jax.ShapeDtypeStruct((B, H, S, LANES), BF16),
        scratch_shapes=([pltpu.VMEM((S, LANES), F32)]
                        + [pltpu.VMEM((S, LANES), BF16)] * 6
                        + [pltpu.VMEM((S, LANES), F32)] * 3),
        compiler_params=pltpu.CompilerParams(
            dimension_semantics=("arbitrary", "arbitrary"),
            vmem_limit_bytes=VMEM_LIMIT_SMALL),
        name="dilated_attention",
    )(q, k, v)


def _ffn_kernel(x_ref, om_ref, od_ref, gm_ref, gd_ref, wo_ref, gf_ref,
                wg_ref, wu_ref, wd_ref, gl_ref, out_ref):
    tile = x_ref.shape[1]
    lane = lax.broadcasted_iota(jnp.int32, (tile, LANES), 1)
    low_head = lane < HEAD_DIM

    def heads(ref):
        pairs = []
        for p in range(ref.shape[1] // 2):
            a = ref[0, 2 * p].astype(F32)
            b = ref[0, 2 * p + 1].astype(F32)
            pairs.append(jnp.where(low_head, a, pltpu.roll(b, HEAD_DIM, 1)))
        return jnp.concatenate(pairs, axis=1)

    mixed = jnp.concatenate([_rms(heads(om_ref), gm_ref[...]),
                             _rms(heads(od_ref), gd_ref[...])], axis=1).astype(BF16)
    x1 = x_ref[0] + jnp.dot(mixed, wo_ref[...], preferred_element_type=F32)
    h = _rms(x1, gf_ref[...]).astype(BF16)
    x2 = x1
    for c in range(D_FF // FFN_CHUNK):
        cols = slice(c * FFN_CHUNK, (c + 1) * FFN_CHUNK)
        g = jnp.dot(h, wg_ref[:, cols], preferred_element_type=F32)
        u = jnp.dot(h, wu_ref[:, cols], preferred_element_type=F32)
        act = (g * jax.nn.sigmoid(g) * u).astype(BF16)
        x2 = x2 + jnp.dot(act, wd_ref[cols, :], preferred_element_type=F32)
    out_ref[0] = _rms(x2, gl_ref[...])


def _ffn_call(x, o_moba, o_dil, moba_out_norm, dil_out_norm, w_out, ffn_norm,
              w_gate, w_up, w_down, final_norm):
    B, S, D = x.shape
    tile = TOKEN_TILE
    half = N_HEADS_MOBA * HEAD_DIM

    def const(shape):
        return pl.BlockSpec(shape, lambda b, s: (0,) * len(shape), pipeline_mode=pl.Buffered(1))

    row_spec = pl.BlockSpec((1, tile, D), lambda b, s: (b, s, 0))
    head_spec = pl.BlockSpec((1, N_HEADS_MOBA, tile, LANES), lambda b, s: (b, 0, s, 0))
    return pl.pallas_call(
        _ffn_kernel,
        grid=(B, S // tile),
        in_specs=[row_spec, head_spec, head_spec,
                  const((1, half)), const((1, half)), const((D, D)), const((1, D)),
                  const((D, D_FF)), const((D, D_FF)), const((D_FF, D)), const((1, D))],
        out_specs=row_spec,
        out_shape=jax.ShapeDtypeStruct((B, S, D), F32),
        compiler_params=pltpu.CompilerParams(
            dimension_semantics=("arbitrary", "arbitrary"),
            vmem_limit_bytes=VMEM_LIMIT_FFN),
        name="outproj_ffn",
    )(x, o_moba, o_dil, moba_out_norm.reshape(1, half), dil_out_norm.reshape(1, half),
      w_out, ffn_norm.reshape(1, D), w_gate, w_up, w_down, final_norm.reshape(1, D))


def _rope_tables(S):
    inv_freq = ROPE_THETA ** (-jnp.arange(0, HEAD_DIM, 2, dtype=F32) / HEAD_DIM)
    ang = jnp.arange(S, dtype=F32)[:, None] * inv_freq[None, :]
    cos = jnp.concatenate([jnp.cos(ang)] * 4, axis=-1)
    sin = jnp.concatenate([-jnp.sin(ang), jnp.sin(ang)] * 2, axis=-1)
    return cos, sin


def kernel(x, attn_norm, w_in, moba_out_norm, dil_out_norm, w_out, ffn_norm, w_gate, w_up,
           w_down, final_norm):
    B, S, D = x.shape
    assert D == D_MODEL and S % TOKEN_TILE == 0 and S % (16 * WINDOW_LEN) == 0
    assert S // MOBA_BLOCK <= LANES - ONEHOT_LANE0
    cos, sin = _rope_tables(S)
    v0 = 2 * D_MODEL
    wv_moba_t = w_in[:, v0:v0 + N_HEADS_MOBA * HEAD_DIM].T.astype(BF16)
    qm, km, vmt, qd, kd, vd = _qkv_call(x, attn_norm, w_in.astype(BF16), wv_moba_t, cos, sin)
    o_moba = _moba_call(qm, km, vmt)
    o_dil = _dilated_call(qd, kd, vd)
    return _ffn_call(x, o_moba, o_dil, moba_out_norm, dil_out_norm, w_out.astype(BF16),
                     ffn_norm, w_gate.astype(BF16), w_up.astype(BF16), w_down.astype(BF16),
                     final_norm)
```

```python
import functools
import math

import jax
import jax.numpy as jnp
from jax import lax
from jax.experimental import pallas as pl
from jax.experimental.pallas import tpu as pltpu

F32 = jnp.float32
BF16 = jnp.bfloat16

D_MODEL = 1024
HEAD_DIM = 64
N_HEADS = D_MODEL // HEAD_DIM
N_HEADS_MOBA = N_HEADS // 2
N_HEADS_DIL = N_HEADS - N_HEADS_MOBA
MOBA_BLOCK = 256
MOBA_TOPK = 3
DILATIONS = (1, 4, 16)
WINDOW_LEN = 128
D_FF = 2816
ROPE_THETA = 10000.0
RMS_EPS = 1e-6
NEG_INF = -1e30
ATTN_SCALE = 1.0 / math.sqrt(HEAD_DIM)

LANES = 128
SUM_LANE = HEAD_DIM
ONEHOT_LANE0 = HEAD_DIM

TOKEN_TILE = 512
FFN_CHUNK = 1408
MERGE_ROWS = 256
GATE_CHUNK = 1024
VMEM_LIMIT_SMALL = 40 * 1024 * 1024
VMEM_LIMIT_FFN = 56 * 1024 * 1024

_NT = (((1,), (1,)), ((), ()))
_TN = (((0,), (0,)), ((), ()))


def _rms(y, gain):
    return y * lax.rsqrt(jnp.mean(y * y, axis=-1, keepdims=True) + RMS_EPS) * gain


def _qkv_kernel(x_ref, g_ref, w_ref, wvt_ref, cos_ref, sin_ref,
                qm_ref, km_ref, vmt_ref, qd_ref, kd_ref, vd_ref):
    tile = x_ref.shape[1]
    moba_width = N_HEADS_MOBA * HEAD_DIM
    h = _rms(x_ref[0], g_ref[...]).astype(BF16)
    cos = cos_ref[...]
    sin = sin_ref[...]
    lane = lax.broadcasted_iota(jnp.int32, (tile, LANES), 1)
    row = lax.broadcasted_iota(jnp.int32, (tile, LANES), 0)
    low_head = lane < HEAD_DIM
    first_half = (lane & (HEAD_DIM - 1)) < HEAD_DIM // 2
    block_of_row = (pl.program_id(1) * tile + row) // MOBA_BLOCK
    onehot = jnp.where(lane - ONEHOT_LANE0 == block_of_row, 1.0, 0.0)
    ones_col = jnp.where(lane == SUM_LANE, 1.0, 0.0)
    zeros = jnp.zeros((tile, LANES), F32)

    def rope(y):
        rot = jnp.where(first_half, pltpu.roll(y, LANES - HEAD_DIM // 2, 1),
                        pltpu.roll(y, HEAD_DIM // 2, 1))
        return y * cos + rot * sin

    def split(y, pad):
        return (jnp.where(low_head, y, pad),
                jnp.where(low_head, pltpu.roll(y, HEAD_DIM, 1), pad))

    q = jnp.dot(h, w_ref[:, 0:D_MODEL], preferred_element_type=F32)
    k = jnp.dot(h, w_ref[:, D_MODEL:2 * D_MODEL], preferred_element_type=F32)
    v = jnp.dot(h, w_ref[:, 2 * D_MODEL + moba_width:3 * D_MODEL], preferred_element_type=F32)
    for pair in range(N_HEADS // 2):
        cols = slice(pair * LANES, (pair + 1) * LANES)
        moba = pair < N_HEADS_MOBA // 2
        q_ref, k_ref = (qm_ref, km_ref) if moba else (qd_ref, kd_ref)
        head0 = 2 * pair - (0 if moba else N_HEADS_MOBA)
        qs = split(rope(q[:, cols]) * ATTN_SCALE, zeros)
        ks = split(rope(k[:, cols]), onehot if moba else zeros)
        for a in range(2):
            q_ref[0, head0 + a] = qs[a].astype(BF16)
            k_ref[0, head0 + a] = ks[a].astype(BF16)
        if not moba:
            vs = split(v[:, head0 * HEAD_DIM:head0 * HEAD_DIM + LANES], ones_col)
            for a in range(2):
                vd_ref[0, head0 + a] = vs[a].astype(BF16)

    vt = lax.dot_general(wvt_ref[...], h, _NT, preferred_element_type=F32)
    sub = lax.broadcasted_iota(jnp.int32, (LANES - HEAD_DIM, tile), 0)
    ones_row = jnp.where(sub == SUM_LANE - HEAD_DIM, 1.0, 0.0)
    for head in range(N_HEADS_MOBA):
        slab = jnp.concatenate([vt[head * HEAD_DIM:(head + 1) * HEAD_DIM, :], ones_row],
                               axis=0).astype(BF16)
        for c in range(tile // MOBA_BLOCK):
            vmt_ref[0, head, c] = slab[:, c * MOBA_BLOCK:(c + 1) * MOBA_BLOCK]


def _qkv_call(x, attn_norm, w_in, wv_moba_t, cos, sin):
    B, S, D = x.shape
    tile = TOKEN_TILE
    H = N_HEADS_MOBA
    head_shape = jax.ShapeDtypeStruct((B, H, S, LANES), BF16)
    head_spec = pl.BlockSpec((1, H, tile, LANES), lambda b, s: (b, 0, s, 0))
    vt_shape = jax.ShapeDtypeStruct((B, H, S // MOBA_BLOCK, LANES, MOBA_BLOCK), BF16)
    vt_spec = pl.BlockSpec((1, H, tile // MOBA_BLOCK, LANES, MOBA_BLOCK),
                           lambda b, s: (b, 0, s, 0, 0))
    return pl.pallas_call(
        _qkv_kernel,
        grid=(B, S // tile),
        in_specs=[
            pl.BlockSpec((1, tile, D), lambda b, s: (b, s, 0)),
            pl.BlockSpec((1, D), lambda b, s: (0, 0)),
            pl.BlockSpec((D, 3 * D), lambda b, s: (0, 0), pipeline_mode=pl.Buffered(1)),
            pl.BlockSpec(wv_moba_t.shape, lambda b, s: (0, 0), pipeline_mode=pl.Buffered(1)),
            pl.BlockSpec((tile, LANES), lambda b, s: (s, 0)),
            pl.BlockSpec((tile, LANES), lambda b, s: (s, 0)),
        ],
        out_specs=[head_spec, head_spec, vt_spec, head_spec, head_spec, head_spec],
        out_shape=[head_shape, head_shape, vt_shape, head_shape, head_shape, head_shape],
        compiler_params=pltpu.CompilerParams(
            dimension_semantics=("arbitrary", "arbitrary"),
            vmem_limit_bytes=VMEM_LIMIT_SMALL),
        name="qkv_rope",
    )(x, attn_norm.reshape(1, D), w_in, wv_moba_t, cos, sin)


def _moba_kernel(q_ref, k_ref, vt_ref, o_ref, qb_ref, s_ref):
    blk = MOBA_BLOCK
    S = k_ref.shape[2]
    n_blocks = S // blk

    kmean = jnp.concatenate(
        [jnp.mean(k_ref[0, 0, n * blk:(n + 1) * blk, :].astype(F32), axis=0, keepdims=True)
         for n in range(n_blocks)], axis=0)
    sub = lax.broadcasted_iota(jnp.int32, (n_blocks, LANES), 0)
    lane = lax.broadcasted_iota(jnp.int32, (n_blocks, LANES), 1)
    place = jnp.where(lane == sub + ONEHOT_LANE0, 1.0, 0.0).astype(BF16)
    gc = GATE_CHUNK
    n_idx = lax.broadcasted_iota(jnp.int32, (n_blocks, gc), 0)
    pos = lax.broadcasted_iota(jnp.int32, (n_blocks, gc), 1)
    for c in range(S // gc):
        q = q_ref[0, 0, c * gc:(c + 1) * gc, :]
        gate_t = lax.dot_general(kmean, q.astype(F32), _NT, precision=lax.Precision.HIGHEST,
                                 preferred_element_type=F32)
        own = (pos + c * gc) // blk
        past = n_idx < own
        gate_t = jnp.where(past, gate_t, NEG_INF)
        rank = jnp.zeros((n_blocks, gc), jnp.int32)
        for n in range(n_blocks):
            other = gate_t[n:n + 1, :]
            ahead = (other > gate_t) | ((other == gate_t) & (n < n_idx))
            rank = rank + ahead.astype(jnp.int32)
        visible = ((rank < MOBA_TOPK) & past) | (n_idx == own)
        bias_t = jnp.where(visible, 0.0, NEG_INF).astype(BF16)
        bias = lax.dot_general(bias_t, place, _TN, preferred_element_type=F32)
        qb_ref[c * gc:(c + 1) * gc, :] = (q.astype(F32) + bias).astype(BF16)

    key_idx = lax.broadcasted_iota(jnp.int32, (blk, blk), 0)
    qry_idx = lax.broadcasted_iota(jnp.int32, (blk, blk), 1)
    causal = key_idx <= qry_idx
    for own in range(n_blocks):
        qb = qb_ref[own * blk:(own + 1) * blk, :]
        buf = own % 2
        m8 = None
        for j in range(own + 1):
            s = lax.dot_general(k_ref[0, 0, j * blk:(j + 1) * blk, :], qb, _NT,
                                preferred_element_type=F32)
            if j == own:
                s = jnp.where(causal, s, NEG_INF)
            s_ref[buf, j * blk:(j + 1) * blk, :] = s
            s8 = jnp.max(s.reshape(blk // 8, 8, blk), axis=0)
            m8 = s8 if m8 is None else jnp.maximum(m8, s8)
        m = jnp.max(m8, axis=0, keepdims=True)
        acc_t = None
        for j in range(own + 1):
            p = jnp.exp(s_ref[buf, j * blk:(j + 1) * blk, :] - m).astype(BF16)
            part = jnp.dot(vt_ref[0, 0, j], p, preferred_element_type=F32)
            acc_t = part if acc_t is None else acc_t + part
        out_t = acc_t / acc_t[SUM_LANE:SUM_LANE + 1, :]
        o_ref[0, 0, own * blk:(own + 1) * blk, :] = out_t.T.astype(BF16)


def _moba_call(q, k, vt):
    B, H, S, _ = q.shape
    blk = MOBA_BLOCK
    assert S % GATE_CHUNK == 0
    seq_spec = pl.BlockSpec((1, 1, S, LANES), lambda b, h: (b, h, 0, 0))
    vt_spec = pl.BlockSpec((1, 1, S // blk, LANES, blk), lambda b, h: (b, h, 0, 0, 0))
    return pl.pallas_call(
        _moba_kernel,
        grid=(B, H),
        in_specs=[seq_spec, seq_spec, vt_spec],
        out_specs=seq_spec,
        out_shape=jax.ShapeDtypeStruct((B, H, S, LANES), BF16),
        scratch_shapes=[pltpu.VMEM((S, LANES), BF16),
                        pltpu.VMEM((2, S, blk), F32)],
        compiler_params=pltpu.CompilerParams(
            dimension_semantics=("arbitrary", "arbitrary"),
            vmem_limit_bytes=VMEM_LIMIT_SMALL),
        name="moba_attention",
    )(q, k, vt)


def _dilated_kernel(q_ref, k_ref, v_ref, o_ref, stage_ref,
                    q4_ref, k4_ref, v4_ref, q16_ref, k16_ref, v16_ref,
                    acc1_ref, acc4_ref, acc16_ref, max1_ref, max4_ref, max16_ref):
    S = q_ref.shape[2]
    L = WINDOW_LEN
    q1_ref, k1_ref, v1_ref = q_ref.at[0, 0], k_ref.at[0, 0], v_ref.at[0, 0]

    for src, dst4, dst16 in ((q1_ref, q4_ref, q16_ref), (k1_ref, k4_ref, k16_ref),
                             (v1_ref, v4_ref, v16_ref)):
        stage_ref[...] = src[...].astype(F32)
        for d, dst in ((4, dst4), (16, dst16)):
            n = S // d
            for r in range(d):
                dst[r * n:(r + 1) * n, :] = stage_ref[pl.ds(r, n, stride=d), :].astype(BF16)

    r_idx = lax.broadcasted_iota(jnp.int32, (L, 2 * L), 0)
    c_idx = lax.broadcasted_iota(jnp.int32, (L, 2 * L), 1)
    band = ((c_idx < L) & (c_idx >= r_idx)) | ((c_idx >= L) & (c_idx - L <= r_idx))
    causal = (lax.broadcasted_iota(jnp.int32, (L, L), 1)
              <= lax.broadcasted_iota(jnp.int32, (L, L), 0))

    for d, qd_ref, kd_ref, vd_ref, acc_ref, max_ref in (
            (1, q1_ref, k1_ref, v1_ref, acc1_ref, max1_ref),
            (4, q4_ref, k4_ref, v4_ref, acc4_ref, max4_ref),
            (16, q16_ref, k16_ref, v16_ref, acc16_ref, max16_ref)):
        n = S // d
        for residue in range(d):
            for t in range(n // L):
                base = residue * n + t * L
                first = base - L if t > 0 else base
                keys = slice(first, base + L)
                s = lax.dot_general(qd_ref[base:base + L, :], kd_ref[keys, :], _NT,
                                    preferred_element_type=F32)
                s = jnp.where(band if t > 0 else causal, s, NEG_INF)
                m = jnp.max(s, axis=1, keepdims=True)
                acc = jnp.dot(jnp.exp(s - m).astype(BF16), vd_ref[keys, :],
                              preferred_element_type=F32)
                rows = pl.ds(t * L * d + residue, L, stride=d) if d > 1 else slice(base, base + L)
                acc_ref[rows, :] = acc
                max_ref[rows, :] = jnp.broadcast_to(m, (L, LANES))

    for c in range(S // MERGE_ROWS):
        rows = slice(c * MERGE_ROWS, (c + 1) * MERGE_ROWS)
        maxes = [ref[rows, :] for ref in (max1_ref, max4_ref, max16_ref)]
        m = jnp.maximum(jnp.maximum(maxes[0], maxes[1]), maxes[2])
        total = sum(jnp.exp(mi - m) * ref[rows, :]
                    for mi, ref in zip(maxes, (acc1_ref, acc4_ref, acc16_ref)))
        o_ref[0, 0, rows, :] = (total / total[:, SUM_LANE:SUM_LANE + 1]).astype(BF16)


def _dilated_call(q, k, v):
    B, H, S, _ = q.shape
    seq_spec = pl.BlockSpec((1, 1, S, LANES), lambda b, h: (b, h, 0, 0))
    return pl.pallas_call(
        _dilated_kernel,
        grid=(B, H),
        in_specs=[seq_spec] * 3,
        out_specs=seq_spec,
        out_shape=jax.ShapeDtypeStruct((B, H, S, LANES), BF16),
        scratch_shapes=([pltpu.VMEM((S, LANES), F32)]
                        + [pltpu.VMEM((S, LANES), BF16)] * 6
                        + [pltpu.VMEM((S, LANES), F32)] * 6),
        compiler_params=pltpu.CompilerParams(
            dimension_semantics=("arbitrary", "arbitrary"),
            vmem_limit_bytes=VMEM_LIMIT_FFN),
        name="dilated_attention",
    )(q, k, v)


def _ffn_kernel(x_ref, om_ref, od_ref, gm_ref, gd_ref, wo_ref, gf_ref,
                wg_ref, wu_ref, wd_ref, gl_ref, out_ref):
    tile = x_ref.shape[1]
    lane = lax.broadcasted_iota(jnp.int32, (tile, LANES), 1)
    low_head = lane < HEAD_DIM

    def heads(ref):
        pairs = []
        for p in range(ref.shape[1] // 2):
            a = ref[0, 2 * p].astype(F32)
            b = ref[0, 2 * p + 1].astype(F32)
            pairs.append(jnp.where(low_head, a, pltpu.roll(b, HEAD_DIM, 1)))
        return jnp.concatenate(pairs, axis=1)

    mixed = jnp.concatenate([_rms(heads(om_ref), gm_ref[...]),
                             _rms(heads(od_ref), gd_ref[...])], axis=1).astype(BF16)
    x1 = x_ref[0] + jnp.dot(mixed, wo_ref[...], preferred_element_type=F32)
    h = _rms(x1, gf_ref[...]).astype(BF16)
    x2 = x1
    for c in range(D_FF // FFN_CHUNK):
        cols = slice(c * FFN_CHUNK, (c + 1) * FFN_CHUNK)
        g = jnp.dot(h, wg_ref[:, cols], preferred_element_type=F32)
        u = jnp.dot(h, wu_ref[:, cols], preferred_element_type=F32)
        act = (g * jax.nn.sigmoid(g) * u).astype(BF16)
        x2 = x2 + jnp.dot(act, wd_ref[cols, :], preferred_element_type=F32)
    out_ref[0] = _rms(x2, gl_ref[...])


def _ffn_call(x, o_moba, o_dil, moba_out_norm, dil_out_norm, w_out, ffn_norm,
              w_gate, w_up, w_down, final_norm):
    B, S, D = x.shape
    tile = TOKEN_TILE
    half = N_HEADS_MOBA * HEAD_DIM

    def const(shape):
        return pl.BlockSpec(shape, lambda b, s: (0,) * len(shape), pipeline_mode=pl.Buffered(1))

    row_spec = pl.BlockSpec((1, tile, D), lambda b, s: (b, s, 0))
    head_spec = pl.BlockSpec((1, N_HEADS_MOBA, tile, LANES), lambda b, s: (b, 0, s, 0))
    return pl.pallas_call(
        _ffn_kernel,
        grid=(B, S // tile),
        in_specs=[row_spec, head_spec, head_spec,
                  const((1, half)), const((1, half)), const((D, D)), const((1, D)),
                  const((D, D_FF)), const((D, D_FF)), const((D_FF, D)), const((1, D))],
        out_specs=row_spec,
        out_shape=jax.ShapeDtypeStruct((B, S, D), F32),
        compiler_params=pltpu.CompilerParams(
            dimension_semantics=("arbitrary", "arbitrary"),
            vmem_limit_bytes=VMEM_LIMIT_FFN),
        name="outproj_ffn",
    )(x, o_moba, o_dil, moba_out_norm.reshape(1, half), dil_out_norm.reshape(1, half),
      w_out, ffn_norm.reshape(1, D), w_gate, w_up, w_down, final_norm.reshape(1, D))


def _rope_tables(S):
    inv_freq = ROPE_THETA ** (-jnp.arange(0, HEAD_DIM, 2, dtype=F32) / HEAD_DIM)
    ang = jnp.arange(S, dtype=F32)[:, None] * inv_freq[None, :]
    cos = jnp.concatenate([jnp.cos(ang)] * 4, axis=-1)
    sin = jnp.concatenate([-jnp.sin(ang), jnp.sin(ang)] * 2, axis=-1)
    return cos, sin


def kernel(x, attn_norm, w_in, moba_out_norm, dil_out_norm, w_out, ffn_norm, w_gate, w_up,
           w_down, final_norm):
    B, S, D = x.shape
    assert D == D_MODEL and S % TOKEN_TILE == 0 and S % (16 * WINDOW_LEN) == 0
    assert S // MOBA_BLOCK <= LANES - ONEHOT_LANE0
    cos, sin = _rope_tables(S)
    v0 = 2 * D_MODEL
    wv_moba_t = w_in[:, v0:v0 + N_HEADS_MOBA * HEAD_DIM].T.astype(BF16)
    qm, km, vmt, qd, kd, vd = _qkv_call(x, attn_norm, w_in.astype(BF16), wv_moba_t, cos, sin)
    o_moba = _moba_call(qm, km, vmt)
    o_dil = _dilated_call(qd, kd, vd)
    return _ffn_call(x, o_moba, o_dil, moba_out_norm, dil_out_norm, w_out.astype(BF16),
                     ffn_norm, w_gate.astype(BF16), w_up.astype(BF16), w_down.astype(BF16),
                     final_norm)
```

```python
import functools
import math

import jax
import jax.numpy as jnp
from jax import lax
from jax.experimental import pallas as pl
from jax.experimental.pallas import tpu as pltpu

F32 = jnp.float32
BF16 = jnp.bfloat16

D_MODEL = 1024
HEAD_DIM = 64
N_HEADS = D_MODEL // HEAD_DIM
N_HEADS_MOBA = N_HEADS // 2
N_HEADS_DIL = N_HEADS - N_HEADS_MOBA
MOBA_BLOCK = 256
MOBA_TOPK = 3
DILATIONS = (1, 4, 16)
WINDOW_LEN = 128
D_FF = 2816
ROPE_THETA = 10000.0
RMS_EPS = 1e-6
NEG_INF = -1e30
ATTN_SCALE = 1.0 / math.sqrt(HEAD_DIM)
Q_SCALE = ATTN_SCALE * math.log2(math.e)

LANES = 128
SUM_LANE = HEAD_DIM
VT_ROWS = HEAD_DIM + 16
ONEHOT_LANE0 = HEAD_DIM

TOKEN_TILE = 512
FFN_CHUNK = 1408
MERGE_ROWS = 256
GATE_CHUNK = 1024
VMEM_LIMIT_SMALL = 40 * 1024 * 1024
VMEM_LIMIT_FFN = 56 * 1024 * 1024

_NT = (((1,), (1,)), ((), ()))
_TN = (((0,), (0,)), ((), ()))


def _rms(y, gain):
    return y * lax.rsqrt(jnp.mean(y * y, axis=-1, keepdims=True) + RMS_EPS) * gain


def _qkv_kernel(x_ref, g_ref, w_ref, wvt_ref, cos_ref, sin_ref,
                qm_ref, km_ref, vmt_ref, qd_ref, kd_ref, vd_ref):
    tile = x_ref.shape[1]
    moba_width = N_HEADS_MOBA * HEAD_DIM
    h = _rms(x_ref[0], g_ref[...]).astype(BF16)
    cos = cos_ref[...]
    sin = sin_ref[...]
    lane = lax.broadcasted_iota(jnp.int32, (tile, LANES), 1)
    row = lax.broadcasted_iota(jnp.int32, (tile, LANES), 0)
    low_head = lane < HEAD_DIM
    first_half = (lane & (HEAD_DIM - 1)) < HEAD_DIM // 2
    block_of_row = (pl.program_id(1) * tile + row) // MOBA_BLOCK
    onehot = jnp.where(lane - ONEHOT_LANE0 == block_of_row, 1.0, 0.0)
    ones_col = jnp.where(lane == SUM_LANE, 1.0, 0.0)
    zeros = jnp.zeros((tile, LANES), F32)

    def rope(y):
        rot = jnp.where(first_half, pltpu.roll(y, LANES - HEAD_DIM // 2, 1),
                        pltpu.roll(y, HEAD_DIM // 2, 1))
        return y * cos + rot * sin

    def split(y, pad):
        return (jnp.where(low_head, y, pad),
                jnp.where(low_head, pltpu.roll(y, HEAD_DIM, 1), pad))

    q = jnp.dot(h, w_ref[:, 0:D_MODEL], preferred_element_type=F32)
    k = jnp.dot(h, w_ref[:, D_MODEL:2 * D_MODEL], preferred_element_type=F32)
    v = jnp.dot(h, w_ref[:, 2 * D_MODEL + moba_width:3 * D_MODEL], preferred_element_type=F32)
    for pair in range(N_HEADS // 2):
        cols = slice(pair * LANES, (pair + 1) * LANES)
        moba = pair < N_HEADS_MOBA // 2
        q_ref, k_ref = (qm_ref, km_ref) if moba else (qd_ref, kd_ref)
        head0 = 2 * pair - (0 if moba else N_HEADS_MOBA)
        qs = split(rope(q[:, cols]) * Q_SCALE, zeros)
        ks = split(rope(k[:, cols]), onehot if moba else zeros)
        for a in range(2):
            q_ref[0, head0 + a] = qs[a].astype(BF16)
            k_ref[0, head0 + a] = ks[a].astype(BF16)
        if not moba:
            vs = split(v[:, head0 * HEAD_DIM:head0 * HEAD_DIM + LANES], ones_col)
            for a in range(2):
                vd_ref[0, head0 + a] = vs[a].astype(BF16)

    vt = lax.dot_general(wvt_ref[...], h, _NT, preferred_element_type=F32)
    sub = lax.broadcasted_iota(jnp.int32, (VT_ROWS - HEAD_DIM, tile), 0)
    ones_row = jnp.where(sub == SUM_LANE - HEAD_DIM, 1.0, 0.0)
    for head in range(N_HEADS_MOBA):
        slab = jnp.concatenate([vt[head * HEAD_DIM:(head + 1) * HEAD_DIM, :], ones_row],
                               axis=0).astype(BF16)
        for c in range(tile // MOBA_BLOCK):
            vmt_ref[0, head, c] = slab[:, c * MOBA_BLOCK:(c + 1) * MOBA_BLOCK]


def _qkv_call(x, attn_norm, w_in, wv_moba_t, cos, sin):
    B, S, D = x.shape
    tile = TOKEN_TILE
    H = N_HEADS_MOBA
    head_shape = jax.ShapeDtypeStruct((B, H, S, LANES), BF16)
    head_spec = pl.BlockSpec((1, H, tile, LANES), lambda b, s: (b, 0, s, 0))
    vt_shape = jax.ShapeDtypeStruct((B, H, S // MOBA_BLOCK, VT_ROWS, MOBA_BLOCK), BF16)
    vt_spec = pl.BlockSpec((1, H, tile // MOBA_BLOCK, VT_ROWS, MOBA_BLOCK),
                           lambda b, s: (b, 0, s, 0, 0))
    return pl.pallas_call(
        _qkv_kernel,
        grid=(B, S // tile),
        in_specs=[
            pl.BlockSpec((1, tile, D), lambda b, s: (b, s, 0)),
            pl.BlockSpec((1, D), lambda b, s: (0, 0)),
            pl.BlockSpec((D, 3 * D), lambda b, s: (0, 0), pipeline_mode=pl.Buffered(1)),
            pl.BlockSpec(wv_moba_t.shape, lambda b, s: (0, 0), pipeline_mode=pl.Buffered(1)),
            pl.BlockSpec((tile, LANES), lambda b, s: (s, 0)),
            pl.BlockSpec((tile, LANES), lambda b, s: (s, 0)),
        ],
        out_specs=[head_spec, head_spec, vt_spec, head_spec, head_spec, head_spec],
        out_shape=[head_shape, head_shape, vt_shape, head_shape, head_shape, head_shape],
        compiler_params=pltpu.CompilerParams(
            dimension_semantics=("arbitrary", "arbitrary"),
            vmem_limit_bytes=VMEM_LIMIT_SMALL),
        name="qkv_rope",
    )(x, attn_norm.reshape(1, D), w_in, wv_moba_t, cos, sin)


def _moba_kernel(q_ref, k_ref, vt_ref, o_ref, qb_ref, s_ref, max_ref):
    blk = MOBA_BLOCK
    S = k_ref.shape[2]
    n_blocks = S // blk

    kmean = jnp.concatenate(
        [jnp.mean(k_ref[0, 0, n * blk:(n + 1) * blk, :].astype(F32), axis=0, keepdims=True)
         for n in range(n_blocks)], axis=0)
    sub = lax.broadcasted_iota(jnp.int32, (n_blocks, LANES), 0)
    lane = lax.broadcasted_iota(jnp.int32, (n_blocks, LANES), 1)
    place = jnp.where(lane == sub + ONEHOT_LANE0, 1.0, 0.0).astype(BF16)
    gc = GATE_CHUNK
    n_idx = lax.broadcasted_iota(jnp.int32, (n_blocks, gc), 0)
    pos = lax.broadcasted_iota(jnp.int32, (n_blocks, gc), 1)
    for c in range(S // gc):
        q = q_ref[0, 0, c * gc:(c + 1) * gc, :]
        gate_t = lax.dot_general(kmean, q.astype(F32), _NT, precision=lax.Precision.HIGHEST,
                                 preferred_element_type=F32)
        own = (pos + c * gc) // blk
        past = n_idx < own
        gate_t = jnp.where(past, gate_t, NEG_INF)
        rank = jnp.zeros((n_blocks, gc), jnp.int32)
        for n in range(n_blocks):
            other = gate_t[n:n + 1, :]
            ahead = (other > gate_t) | ((other == gate_t) & (n < n_idx))
            rank = rank + ahead.astype(jnp.int32)
        visible = ((rank < MOBA_TOPK) & past) | (n_idx == own)
        bias_t = jnp.where(visible, 0.0, NEG_INF).astype(BF16)
        bias = lax.dot_general(bias_t, place, _TN, preferred_element_type=F32)
        qb_ref[c * gc:(c + 1) * gc, :] = (q.astype(F32) + bias).astype(BF16)

    def score_pass(own):
        qb = qb_ref[own * blk:(own + 1) * blk, :]
        m8 = None
        for j in range(own + 1):
            s = lax.dot_general(k_ref[0, 0, j * blk:(j + 1) * blk, :], qb, _NT,
                                preferred_element_type=F32)
            if j == own:
                key_idx = lax.broadcasted_iota(jnp.int32, (blk, blk), 0)
                qry_idx = lax.broadcasted_iota(jnp.int32, (blk, blk), 1)
                s = jnp.where(key_idx <= qry_idx, s, NEG_INF)
            s_ref[own % 2, j * blk:(j + 1) * blk, :] = s
            s8 = jnp.max(s.reshape(blk // 8, 8, blk), axis=0)
            m8 = s8 if m8 is None else jnp.maximum(m8, s8)
            yield
        max_ref[own:own + 1, :] = jnp.max(m8, axis=0, keepdims=True)

    def value_pass(own):
        m = max_ref[own:own + 1, :]
        acc_t = None
        for j in range(own + 1):
            p = jnp.exp2(s_ref[own % 2, j * blk:(j + 1) * blk, :] - m).astype(BF16)
            part = jnp.dot(vt_ref[0, 0, j], p, preferred_element_type=F32)
            acc_t = part if acc_t is None else acc_t + part
            yield
        out_t = acc_t / acc_t[SUM_LANE:SUM_LANE + 1, :]
        out_t = jnp.concatenate([out_t, jnp.zeros((LANES - VT_ROWS, blk), F32)], axis=0)
        o_ref[0, 0, own * blk:(own + 1) * blk, :] = out_t.T.astype(BF16)

    def interleave(*steps):
        steps = list(steps)
        while steps:
            for g in list(steps):
                if next(g, StopIteration) is StopIteration:
                    steps.remove(g)

    for t in range(n_blocks + 1):
        @pl.when(pl.program_id(0) > -1 - t)
        def _(t=t):
            parts = []
            if t > 0:
                parts.append(value_pass(t - 1))
            if t < n_blocks:
                parts.append(score_pass(t))
            interleave(*parts)


def _moba_call(q, k, vt):
    B, H, S, _ = q.shape
    blk = MOBA_BLOCK
    assert S % GATE_CHUNK == 0
    seq_spec = pl.BlockSpec((1, 1, S, LANES), lambda b, h: (b, h, 0, 0))
    vt_spec = pl.BlockSpec((1, 1, S // blk, VT_ROWS, blk), lambda b, h: (b, h, 0, 0, 0))
    return pl.pallas_call(
        _moba_kernel,
        grid=(B, H),
        in_specs=[seq_spec, seq_spec, vt_spec],
        out_specs=seq_spec,
        out_shape=jax.ShapeDtypeStruct((B, H, S, LANES), BF16),
        scratch_shapes=[pltpu.VMEM((S, LANES), BF16),
                        pltpu.VMEM((2, S, blk), F32),
                        pltpu.VMEM((S // blk, blk), F32)],
        compiler_params=pltpu.CompilerParams(
            dimension_semantics=("arbitrary", "arbitrary"),
            vmem_limit_bytes=VMEM_LIMIT_SMALL),
        name="moba_attention",
    )(q, k, vt)


def _dilated_kernel(q_ref, k_ref, v_ref, o_ref, stage_ref,
                    q4_ref, k4_ref, v4_ref, q16_ref, k16_ref, v16_ref,
                    acc1_ref, acc4_ref, acc16_ref, max1_ref, max4_ref, max16_ref):
    S = q_ref.shape[2]
    L = WINDOW_LEN
    q1_ref, k1_ref, v1_ref = q_ref.at[0, 0], k_ref.at[0, 0], v_ref.at[0, 0]

    for src, dst4, dst16 in ((q1_ref, q4_ref, q16_ref), (k1_ref, k4_ref, k16_ref),
                             (v1_ref, v4_ref, v16_ref)):
        stage_ref[...] = src[...].astype(F32)
        for d, dst in ((4, dst4), (16, dst16)):
            n = S // d
            for r in range(d):
                dst[r * n:(r + 1) * n, :] = stage_ref[pl.ds(r, n, stride=d), :].astype(BF16)

    r_idx = lax.broadcasted_iota(jnp.int32, (L, 2 * L), 0)
    c_idx = lax.broadcasted_iota(jnp.int32, (L, 2 * L), 1)
    band = ((c_idx < L) & (c_idx >= r_idx)) | ((c_idx >= L) & (c_idx - L <= r_idx))
    causal = (lax.broadcasted_iota(jnp.int32, (L, L), 1)
              <= lax.broadcasted_iota(jnp.int32, (L, L), 0))

    for d, qd_ref, kd_ref, vd_ref, acc_ref, max_ref in (
            (1, q1_ref, k1_ref, v1_ref, acc1_ref, max1_ref),
            (4, q4_ref, k4_ref, v4_ref, acc4_ref, max4_ref),
            (16, q16_ref, k16_ref, v16_ref, acc16_ref, max16_ref)):
        n = S // d
        for residue in range(d):
            for t in range(n // L):
                base = residue * n + t * L
                first = base - L if t > 0 else base
                keys = slice(first, base + L)
                s = lax.dot_general(qd_ref[base:base + L, :], kd_ref[keys, :], _NT,
                                    preferred_element_type=F32)
                s = jnp.where(band if t > 0 else causal, s, NEG_INF)
                m = jnp.max(s, axis=1, keepdims=True)
                acc = jnp.dot(jnp.exp2(s - m).astype(BF16), vd_ref[keys, :],
                              preferred_element_type=F32)
                rows = pl.ds(t * L * d + residue, L, stride=d) if d > 1 else slice(base, base + L)
                acc_ref[rows, :] = acc
                max_ref[rows, :] = jnp.broadcast_to(m, (L, LANES))

    for c in range(S // MERGE_ROWS):
        rows = slice(c * MERGE_ROWS, (c + 1) * MERGE_ROWS)
        maxes = [ref[rows, :] for ref in (max1_ref, max4_ref, max16_ref)]
        m = jnp.maximum(jnp.maximum(maxes[0], maxes[1]), maxes[2])
        total = sum(jnp.exp2(mi - m) * ref[rows, :]
                    for mi, ref in zip(maxes, (acc1_ref, acc4_ref, acc16_ref)))
        o_ref[0, 0, rows, :] = (total / total[:, SUM_LANE:SUM_LANE + 1]).astype(BF16)


def _dilated_call(q, k, v):
    B, H, S, _ = q.shape
    seq_spec = pl.BlockSpec((1, 1, S, LANES), lambda b, h: (b, h, 0, 0))
    return pl.pallas_call(
        _dilated_kernel,
        grid=(B, H),
        in_specs=[seq_spec] * 3,
        out_specs=seq_spec,
        out_shape=jax.ShapeDtypeStruct((B, H, S, LANES), BF16),
        scratch_shapes=([pltpu.VMEM((S, LANES), F32)]
                        + [pltpu.VMEM((S, LANES), BF16)] * 6
                        + [pltpu.VMEM((S, LANES), F32)] * 6),
        compiler_params=pltpu.CompilerParams(
            dimension_semantics=("arbitrary", "arbitrary"),
            vmem_limit_bytes=VMEM_LIMIT_FFN),
        name="dilated_attention",
    )(q, k, v)


def _ffn_kernel(x_ref, om_ref, od_ref, gm_ref, gd_ref, wo_ref, gf_ref,
                wg_ref, wu_ref, wd_ref, gl_ref, out_ref):
    tile = x_ref.shape[1]
    lane = lax.broadcasted_iota(jnp.int32, (tile, LANES), 1)
    low_head = lane < HEAD_DIM

    def heads(ref):
        pairs = []
        for p in range(ref.shape[1] // 2):
            a = ref[0, 2 * p].astype(F32)
            b = ref[0, 2 * p + 1].astype(F32)
            pairs.append(jnp.where(low_head, a, pltpu.roll(b, HEAD_DIM, 1)))
        return jnp.concatenate(pairs, axis=1)

    mixed = jnp.concatenate([_rms(heads(om_ref), gm_ref[...]),
                             _rms(heads(od_ref), gd_ref[...])], axis=1).astype(BF16)
    x1 = x_ref[0] + jnp.dot(mixed, wo_ref[...], preferred_element_type=F32)
    h = _rms(x1, gf_ref[...]).astype(BF16)
    x2 = x1
    for c in range(D_FF // FFN_CHUNK):
        cols = slice(c * FFN_CHUNK, (c + 1) * FFN_CHUNK)
        g = jnp.dot(h, wg_ref[:, cols], preferred_element_type=F32)
        u = jnp.dot(h, wu_ref[:, cols], preferred_element_type=F32)
        act = (g * jax.nn.sigmoid(g) * u).astype(BF16)
        x2 = x2 + jnp.dot(act, wd_ref[cols, :], preferred_element_type=F32)
    out_ref[0] = _rms(x2, gl_ref[...])


def _ffn_call(x, o_moba, o_dil, moba_out_norm, dil_out_norm, w_out, ffn_norm,
              w_gate, w_up, w_down, final_norm):
    B, S, D = x.shape
    tile = TOKEN_TILE
    half = N_HEADS_MOBA * HEAD_DIM

    def const(shape):
        return pl.BlockSpec(shape, lambda b, s: (0,) * len(shape), pipeline_mode=pl.Buffered(1))

    row_spec = pl.BlockSpec((1, tile, D), lambda b, s: (b, s, 0))
    head_spec = pl.BlockSpec((1, N_HEADS_MOBA, tile, LANES), lambda b, s: (b, 0, s, 0))
    return pl.pallas_call(
        _ffn_kernel,
        grid=(B, S // tile),
        in_specs=[row_spec, head_spec, head_spec,
                  const((1, half)), const((1, half)), const((D, D)), const((1, D)),
                  const((D, D_FF)), const((D, D_FF)), const((D_FF, D)), const((1, D))],
        out_specs=row_spec,
        out_shape=jax.ShapeDtypeStruct((B, S, D), F32),
        compiler_params=pltpu.CompilerParams(
            dimension_semantics=("arbitrary", "arbitrary"),
            vmem_limit_bytes=VMEM_LIMIT_FFN),
        name="outproj_ffn",
    )(x, o_moba, o_dil, moba_out_norm.reshape(1, half), dil_out_norm.reshape(1, half),
      w_out, ffn_norm.reshape(1, D), w_gate, w_up, w_down, final_norm.reshape(1, D))


def _rope_tables(S):
    inv_freq = ROPE_THETA ** (-jnp.arange(0, HEAD_DIM, 2, dtype=F32) / HEAD_DIM)
    ang = jnp.arange(S, dtype=F32)[:, None] * inv_freq[None, :]
    cos = jnp.concatenate([jnp.cos(ang)] * 4, axis=-1)
    sin = jnp.concatenate([-jnp.sin(ang), jnp.sin(ang)] * 2, axis=-1)
    return cos, sin


def kernel(x, attn_norm, w_in, moba_out_norm, dil_out_norm, w_out, ffn_norm, w_gate, w_up,
           w_down, final_norm):
    B, S, D = x.shape
    assert D == D_MODEL and S % TOKEN_TILE == 0 and S % (16 * WINDOW_LEN) == 0
    assert S // MOBA_BLOCK <= LANES - ONEHOT_LANE0
    cos, sin = _rope_tables(S)
    v0 = 2 * D_MODEL
    wv_moba_t = w_in[:, v0:v0 + N_HEADS_MOBA * HEAD_DIM].T.astype(BF16)
    qm, km, vmt, qd, kd, vd = _qkv_call(x, attn_norm, w_in.astype(BF16), wv_moba_t, cos, sin)
    o_moba = _moba_call(qm, km, vmt)
    o_dil = _dilated_call(qd, kd, vd)
    return _ffn_call(x, o_moba, o_dil, moba_out_norm, dil_out_norm, w_out.astype(BF16),
                     ffn_norm, w_gate.astype(BF16), w_up.astype(BF16), w_down.astype(BF16),
                     final_norm)
```

```python
import functools
import math

import jax
import jax.numpy as jnp
from jax import lax
from jax.experimental import pallas as pl
from jax.experimental.pallas import tpu as pltpu

F32 = jnp.float32
BF16 = jnp.bfloat16

D_MODEL = 1024
HEAD_DIM = 64
N_HEADS = D_MODEL // HEAD_DIM
N_HEADS_MOBA = N_HEADS // 2
N_HEADS_DIL = N_HEADS - N_HEADS_MOBA
MOBA_BLOCK = 256
MOBA_TOPK = 3
DILATIONS = (1, 4, 16)
WINDOW_LEN = 128
D_FF = 2816
ROPE_THETA = 10000.0
RMS_EPS = 1e-6
NEG_INF = -1e30
ATTN_SCALE = 1.0 / math.sqrt(HEAD_DIM)
Q_SCALE = ATTN_SCALE * math.log2(math.e)

LANES = 128
SUM_LANE = HEAD_DIM
VT_ROWS = HEAD_DIM + 16
ONEHOT_LANE0 = HEAD_DIM

TOKEN_TILE = 512
FFN_CHUNK = 1408
MERGE_ROWS = 256
DILATED_GROUP = 8
GATE_CHUNK = 1024
VMEM_LIMIT_SMALL = 40 * 1024 * 1024
VMEM_LIMIT_FFN = 56 * 1024 * 1024

_NT = (((1,), (1,)), ((), ()))
_TN = (((0,), (0,)), ((), ()))


def _rms(y, gain):
    return y * lax.rsqrt(jnp.mean(y * y, axis=-1, keepdims=True) + RMS_EPS) * gain


def _qkv_kernel(x_ref, g_ref, w_ref, wvt_ref, cos_ref, sin_ref,
                qm_ref, km_ref, vmt_ref, qd_ref, kd_ref, vd_ref):
    tile = x_ref.shape[1]
    moba_width = N_HEADS_MOBA * HEAD_DIM
    h = _rms(x_ref[0], g_ref[...]).astype(BF16)
    cos = cos_ref[...]
    sin = sin_ref[...]
    lane = lax.broadcasted_iota(jnp.int32, (tile, LANES), 1)
    row = lax.broadcasted_iota(jnp.int32, (tile, LANES), 0)
    low_head = lane < HEAD_DIM
    first_half = (lane & (HEAD_DIM - 1)) < HEAD_DIM // 2
    block_of_row = (pl.program_id(1) * tile + row) // MOBA_BLOCK
    onehot = jnp.where(lane - ONEHOT_LANE0 == block_of_row, 1.0, 0.0)
    ones_col = jnp.where(lane == SUM_LANE, 1.0, 0.0)
    zeros = jnp.zeros((tile, LANES), F32)

    def rope(y):
        rot = jnp.where(first_half, pltpu.roll(y, LANES - HEAD_DIM // 2, 1),
                        pltpu.roll(y, HEAD_DIM // 2, 1))
        return y * cos + rot * sin

    def split(y, pad):
        return (jnp.where(low_head, y, pad),
                jnp.where(low_head, pltpu.roll(y, HEAD_DIM, 1), pad))

    q = jnp.dot(h, w_ref[:, 0:D_MODEL], preferred_element_type=F32)
    k = jnp.dot(h, w_ref[:, D_MODEL:2 * D_MODEL], preferred_element_type=F32)
    v = jnp.dot(h, w_ref[:, 2 * D_MODEL + moba_width:3 * D_MODEL], preferred_element_type=F32)
    for pair in range(N_HEADS // 2):
        cols = slice(pair * LANES, (pair + 1) * LANES)
        moba = pair < N_HEADS_MOBA // 2
        q_ref, k_ref = (qm_ref, km_ref) if moba else (qd_ref, kd_ref)
        head0 = 2 * pair - (0 if moba else N_HEADS_MOBA)
        qs = split(rope(q[:, cols]) * Q_SCALE, zeros)
        ks = split(rope(k[:, cols]), onehot if moba else zeros)
        for a in range(2):
            q_ref[0, head0 + a] = qs[a].astype(BF16)
            k_ref[0, head0 + a] = ks[a].astype(BF16)
        if not moba:
            vs = split(v[:, head0 * HEAD_DIM:head0 * HEAD_DIM + LANES], ones_col)
            for a in range(2):
                vd_ref[0, head0 + a] = vs[a].astype(BF16)

    vt = lax.dot_general(wvt_ref[...], h, _NT, preferred_element_type=F32)
    sub = lax.broadcasted_iota(jnp.int32, (VT_ROWS - HEAD_DIM, tile), 0)
    ones_row = jnp.where(sub == SUM_LANE - HEAD_DIM, 1.0, 0.0)
    for head in range(N_HEADS_MOBA):
        slab = jnp.concatenate([vt[head * HEAD_DIM:(head + 1) * HEAD_DIM, :], ones_row],
                               axis=0).astype(BF16)
        for c in range(tile // MOBA_BLOCK):
            vmt_ref[0, head, c] = slab[:, c * MOBA_BLOCK:(c + 1) * MOBA_BLOCK]


def _qkv_call(x, attn_norm, w_in, wv_moba_t, cos, sin):
    B, S, D = x.shape
    tile = TOKEN_TILE
    H = N_HEADS_MOBA
    head_shape = jax.ShapeDtypeStruct((B, H, S, LANES), BF16)
    head_spec = pl.BlockSpec((1, H, tile, LANES), lambda b, s: (b, 0, s, 0))
    vt_shape = jax.ShapeDtypeStruct((B, H, S // MOBA_BLOCK, VT_ROWS, MOBA_BLOCK), BF16)
    vt_spec = pl.BlockSpec((1, H, tile // MOBA_BLOCK, VT_ROWS, MOBA_BLOCK),
                           lambda b, s: (b, 0, s, 0, 0))
    return pl.pallas_call(
        _qkv_kernel,
        grid=(B, S // tile),
        in_specs=[
            pl.BlockSpec((1, tile, D), lambda b, s: (b, s, 0)),
            pl.BlockSpec((1, D), lambda b, s: (0, 0)),
            pl.BlockSpec((D, 3 * D), lambda b, s: (0, 0), pipeline_mode=pl.Buffered(1)),
            pl.BlockSpec(wv_moba_t.shape, lambda b, s: (0, 0), pipeline_mode=pl.Buffered(1)),
            pl.BlockSpec((tile, LANES), lambda b, s: (s, 0)),
            pl.BlockSpec((tile, LANES), lambda b, s: (s, 0)),
        ],
        out_specs=[head_spec, head_spec, vt_spec, head_spec, head_spec, head_spec],
        out_shape=[head_shape, head_shape, vt_shape, head_shape, head_shape, head_shape],
        compiler_params=pltpu.CompilerParams(
            dimension_semantics=("arbitrary", "arbitrary"),
            vmem_limit_bytes=VMEM_LIMIT_SMALL),
        name="qkv_rope",
    )(x, attn_norm.reshape(1, D), w_in, wv_moba_t, cos, sin)


def _moba_kernel(q_ref, k_ref, vt_ref, o_ref, qb_ref, s_ref, max_ref):
    blk = MOBA_BLOCK
    S = k_ref.shape[2]
    n_blocks = S // blk

    kmean = jnp.concatenate(
        [jnp.mean(k_ref[0, 0, n * blk:(n + 1) * blk, :].astype(F32), axis=0, keepdims=True)
         for n in range(n_blocks)], axis=0)
    sub = lax.broadcasted_iota(jnp.int32, (n_blocks, LANES), 0)
    lane = lax.broadcasted_iota(jnp.int32, (n_blocks, LANES), 1)
    place = jnp.where(lane == sub + ONEHOT_LANE0, 1.0, 0.0).astype(BF16)
    k_hi = kmean.astype(BF16)
    k_mid = (kmean - k_hi.astype(F32)).astype(BF16)
    k_lo = (kmean - k_hi.astype(F32) - k_mid.astype(F32)).astype(BF16)
    k_terms = jnp.concatenate([k_hi, k_mid, k_lo], axis=0)
    gc = GATE_CHUNK
    n_idx = lax.broadcasted_iota(jnp.int32, (n_blocks, gc), 0)
    pos = lax.broadcasted_iota(jnp.int32, (n_blocks, gc), 1)
    q_lanes = lax.broadcasted_iota(jnp.int32, (gc, LANES), 1) < HEAD_DIM
    for c in range(S // gc):
        q = q_ref[0, 0, c * gc:(c + 1) * gc, :]
        g3 = lax.dot_general(k_terms, q, _NT, preferred_element_type=F32)
        gate_t = g3[:n_blocks] + g3[n_blocks:2 * n_blocks] + g3[2 * n_blocks:]
        own = (pos + c * gc) // blk
        past = n_idx < own
        gate_t = jnp.where(past, gate_t, NEG_INF)
        chosen = jnp.zeros((n_blocks, gc), F32)
        for _ in range(MOBA_TOPK):
            best = jnp.max(gate_t, axis=0, keepdims=True)
            first = jnp.min(jnp.where(gate_t == best, n_idx, n_blocks), axis=0, keepdims=True)
            hit = n_idx == first
            chosen = jnp.where(hit, 1.0, chosen)
            gate_t = jnp.where(hit, -jnp.inf, gate_t)
        visible = ((chosen > 0.0) & past) | (n_idx == own)
        bias_t = jnp.where(visible, 0.0, NEG_INF).astype(BF16)
        bias = lax.dot_general(bias_t, place, _TN, preferred_element_type=F32)
        qb_ref[c * gc:(c + 1) * gc, :] = jnp.where(q_lanes, q, bias.astype(BF16))

    def score_pass(own):
        qb = qb_ref[own * blk:(own + 1) * blk, :]
        m8 = None
        for j in range(own + 1):
            s = lax.dot_general(k_ref[0, 0, j * blk:(j + 1) * blk, :], qb, _NT,
                                preferred_element_type=F32)
            if j == own:
                key_idx = lax.broadcasted_iota(jnp.int32, (blk, blk), 0)
                qry_idx = lax.broadcasted_iota(jnp.int32, (blk, blk), 1)
                s = jnp.where(key_idx <= qry_idx, s, NEG_INF)
            s_ref[own % 2, j * blk:(j + 1) * blk, :] = s
            s8 = jnp.max(s.reshape(blk // 8, 8, blk), axis=0)
            m8 = s8 if m8 is None else jnp.maximum(m8, s8)
            yield
        max_ref[own:own + 1, :] = jnp.max(m8, axis=0, keepdims=True)

    def value_pass(own):
        m = max_ref[own:own + 1, :]
        acc_t = None
        for j in range(own + 1):
            p = jnp.exp2(s_ref[own % 2, j * blk:(j + 1) * blk, :] - m).astype(BF16)
            part = jnp.dot(vt_ref[0, 0, j], p, preferred_element_type=F32)
            acc_t = part if acc_t is None else acc_t + part
            yield
        out_t = acc_t / acc_t[SUM_LANE:SUM_LANE + 1, :]
        out_t = jnp.concatenate([out_t, jnp.zeros((LANES - VT_ROWS, blk), F32)], axis=0)
        o_ref[0, 0, own * blk:(own + 1) * blk, :] = out_t.T.astype(BF16)

    def interleave(*steps):
        steps = list(steps)
        while steps:
            for g in list(steps):
                if next(g, StopIteration) is StopIteration:
                    steps.remove(g)

    for t in range(n_blocks + 1):
        @pl.when(pl.program_id(0) > -1 - t)
        def _(t=t):
            parts = []
            if t > 0:
                parts.append(value_pass(t - 1))
            if t < n_blocks:
                parts.append(score_pass(t))
            interleave(*parts)


def _moba_call(q, k, vt):
    B, H, S, _ = q.shape
    blk = MOBA_BLOCK
    assert S % GATE_CHUNK == 0
    seq_spec = pl.BlockSpec((1, 1, S, LANES), lambda b, h: (b, h, 0, 0))
    vt_spec = pl.BlockSpec((1, 1, S // blk, VT_ROWS, blk), lambda b, h: (b, h, 0, 0, 0))
    return pl.pallas_call(
        _moba_kernel,
        grid=(B, H),
        in_specs=[seq_spec, seq_spec, vt_spec],
        out_specs=seq_spec,
        out_shape=jax.ShapeDtypeStruct((B, H, S, LANES), BF16),
        scratch_shapes=[pltpu.VMEM((S, LANES), BF16),
                        pltpu.VMEM((2, S, blk), F32),
                        pltpu.VMEM((S // blk, blk), F32)],
        compiler_params=pltpu.CompilerParams(
            dimension_semantics=("arbitrary", "arbitrary"),
            vmem_limit_bytes=VMEM_LIMIT_SMALL),
        name="moba_attention",
    )(q, k, vt)


def _dilated_kernel(q_ref, k_ref, v_ref, o_ref, stage_ref, stage4_ref,
                    q4_ref, k4_ref, v4_ref, q16_ref, k16_ref, v16_ref,
                    acc1_ref, acc4_ref, acc16_ref, max1_ref, max4_ref, max16_ref,
                    s_ref, m_ref):
    S = q_ref.shape[2]
    L = WINDOW_LEN
    q1_ref, k1_ref, v1_ref = q_ref.at[0, 0], k_ref.at[0, 0], v_ref.at[0, 0]

    n4, n16 = S // 4, S // 16

    def deinterleave(src, dst4, dst16):
        stage_ref[...] = src[...].astype(F32)
        for r in range(4):
            x = stage_ref[pl.ds(r, n4, stride=4), :]
            dst4[r * n4:(r + 1) * n4, :] = x.astype(BF16)
            stage4_ref[r * n4:(r + 1) * n4, :] = x
        for r in range(16):
            x = stage4_ref[pl.ds((r % 4) * n4 + r // 4, n16, stride=4), :]
            dst16[r * n16:(r + 1) * n16, :] = x.astype(BF16)
        yield

    acc16t_ref, max16t_ref = stage_ref, stage4_ref

    def order16():
        for r in range(4):
            acc16_ref[pl.ds(r, n4, stride=4), :] = acc16t_ref[r * n4:(r + 1) * n4, :]
            max16_ref[pl.ds(r, n4, stride=4), :] = max16t_ref[r * n4:(r + 1) * n4, :]
        yield

    tiles = []
    for d, qd_ref, kd_ref, vd_ref, acc_ref, max_ref in (
            (1, q1_ref, k1_ref, v1_ref, acc1_ref, max1_ref),
            (4, q4_ref, k4_ref, v4_ref, acc4_ref, max4_ref),
            (16, q16_ref, k16_ref, v16_ref, acc16t_ref, max16t_ref)):
        n = S // d
        for residue in range(d):
            for t in range(n // L):
                base = residue * n + t * L
                keys = slice(base - L if t > 0 else base, base + L)
                if d == 1:
                    rows = slice(base, base + L)
                elif d == 4:
                    rows = pl.ds(t * L * 4 + residue, L, stride=4)
                else:
                    rows = pl.ds((residue % 4) * n4 + t * L * 4 + residue // 4, L, stride=4)
                tiles.append((qd_ref, kd_ref, vd_ref, acc_ref, max_ref, base, keys, rows))
    group = DILATED_GROUP
    n_groups = len(tiles) // group

    def score_pass(g):
        for i, (qd_ref, kd_ref, _, _, max_ref, base, keys, rows) in enumerate(
                tiles[g * group:(g + 1) * group]):
            width = keys.stop - keys.start
            s = lax.dot_general(qd_ref[base:base + L, :], kd_ref[keys, :], _NT,
                                preferred_element_type=F32)
            r_idx = lax.broadcasted_iota(jnp.int32, (L, width), 0)
            c_idx = lax.broadcasted_iota(jnp.int32, (L, width), 1)
            if width == 2 * L:
                seen = ((c_idx < L) & (c_idx >= r_idx)) | ((c_idx >= L) & (c_idx - L <= r_idx))
            else:
                seen = c_idx <= r_idx
            s = jnp.where(seen, s, NEG_INF)
            m = jnp.max(s, axis=1, keepdims=True)
            m = jnp.broadcast_to(m, (L, LANES))
            s_ref[g % 2, i * L:(i + 1) * L, 0:width] = s
            m_ref[g % 2, i * L:(i + 1) * L, :] = m
            max_ref[rows, :] = m
            yield

    def value_pass(g):
        for i, (_, _, vd_ref, acc_ref, _, _, keys, rows) in enumerate(
                tiles[g * group:(g + 1) * group]):
            width = keys.stop - keys.start
            m = m_ref[g % 2, i * L:(i + 1) * L, :]
            p = jnp.concatenate(
                [jnp.exp2(s_ref[g % 2, i * L:(i + 1) * L, c * L:(c + 1) * L] - m)
                 for c in range(width // L)], axis=1).astype(BF16)
            acc_ref[rows, :] = jnp.dot(p, vd_ref[keys, :], preferred_element_type=F32)
            yield

    def merge():
        for c in range(S // MERGE_ROWS):
            rows = slice(c * MERGE_ROWS, (c + 1) * MERGE_ROWS)
            maxes = [ref[rows, :] for ref in (max1_ref, max4_ref, max16_ref)]
            m = jnp.maximum(jnp.maximum(maxes[0], maxes[1]), maxes[2])
            total = sum(jnp.exp2(mi - m) * ref[rows, :]
                        for mi, ref in zip(maxes, (acc1_ref, acc4_ref, acc16_ref)))
            o_ref[0, 0, rows, :] = (total / total[:, SUM_LANE:SUM_LANE + 1]).astype(BF16)
            yield

    def interleave(*steps):
        steps = list(steps)
        while steps:
            for g in list(steps):
                if next(g, StopIteration) is StopIteration:
                    steps.remove(g)

    copies = [(q1_ref, q4_ref, q16_ref), (k1_ref, k4_ref, k16_ref), (v1_ref, v4_ref, v16_ref)]
    assert len(copies) <= S // L // group
    for t in range(n_groups + 2):
        @pl.when(pl.program_id(0) > -1 - t)
        def _(t=t):
            parts = []
            if 0 < t <= n_groups:
                parts.append(value_pass(t - 1))
            if t < n_groups:
                parts.append(score_pass(t))
            if t < len(copies):
                parts.append(deinterleave(*copies[t]))
            if t == n_groups + 1:
                interleave(order16())
                parts.append(merge())
            interleave(*parts)


def _dilated_call(q, k, v):
    B, H, S, _ = q.shape
    seq_spec = pl.BlockSpec((1, 1, S, LANES), lambda b, h: (b, h, 0, 0))
    return pl.pallas_call(
        _dilated_kernel,
        grid=(B, H),
        in_specs=[seq_spec] * 3,
        out_specs=seq_spec,
        out_shape=jax.ShapeDtypeStruct((B, H, S, LANES), BF16),
        scratch_shapes=([pltpu.VMEM((S, LANES), F32)] * 2
                        + [pltpu.VMEM((S, LANES), BF16)] * 6
                        + [pltpu.VMEM((S, LANES), F32)] * 6
                        + [pltpu.VMEM((2, DILATED_GROUP * WINDOW_LEN, 2 * WINDOW_LEN), F32),
                           pltpu.VMEM((2, DILATED_GROUP * WINDOW_LEN, LANES), F32)]),
        compiler_params=pltpu.CompilerParams(
            dimension_semantics=("arbitrary", "arbitrary"),
            vmem_limit_bytes=VMEM_LIMIT_FFN),
        name="dilated_attention",
    )(q, k, v)


def _ffn_kernel(x_ref, om_ref, od_ref, gm_ref, gd_ref, wo_ref, gf_ref,
                wg_ref, wu_ref, wd_ref, gl_ref, out_ref):
    tile = x_ref.shape[1]
    lane = lax.broadcasted_iota(jnp.int32, (tile, LANES), 1)
    low_head = lane < HEAD_DIM

    def heads(ref):
        pairs = []
        for p in range(ref.shape[1] // 2):
            a = ref[0, 2 * p].astype(F32)
            b = ref[0, 2 * p + 1].astype(F32)
            pairs.append(jnp.where(low_head, a, pltpu.roll(b, HEAD_DIM, 1)))
        return jnp.concatenate(pairs, axis=1)

    mixed = jnp.concatenate([_rms(heads(om_ref), gm_ref[...]),
                             _rms(heads(od_ref), gd_ref[...])], axis=1).astype(BF16)
    x1 = x_ref[0] + jnp.dot(mixed, wo_ref[...], preferred_element_type=F32)
    h = _rms(x1, gf_ref[...]).astype(BF16)
    x2 = x1
    for c in range(D_FF // FFN_CHUNK):
        cols = slice(c * FFN_CHUNK, (c + 1) * FFN_CHUNK)
        g = jnp.dot(h, wg_ref[:, cols], preferred_element_type=F32)
        u = jnp.dot(h, wu_ref[:, cols], preferred_element_type=F32)
        act = (g * jax.nn.sigmoid(g) * u).astype(BF16)
        x2 = x2 + jnp.dot(act, wd_ref[cols, :], preferred_element_type=F32)
    out_ref[0] = _rms(x2, gl_ref[...])


def _ffn_call(x, o_moba, o_dil, moba_out_norm, dil_out_norm, w_out, ffn_norm,
              w_gate, w_up, w_down, final_norm):
    B, S, D = x.shape
    tile = TOKEN_TILE
    half = N_HEADS_MOBA * HEAD_DIM

    def const(shape):
        return pl.BlockSpec(shape, lambda b, s: (0,) * len(shape), pipeline_mode=pl.Buffered(1))

    row_spec = pl.BlockSpec((1, tile, D), lambda b, s: (b, s, 0))
    head_spec = pl.BlockSpec((1, N_HEADS_MOBA, tile, LANES), lambda b, s: (b, 0, s, 0))
    return pl.pallas_call(
        _ffn_kernel,
        grid=(B, S // tile),
        in_specs=[row_spec, head_spec, head_spec,
                  const((1, half)), const((1, half)), const((D, D)), const((1, D)),
                  const((D, D_FF)), const((D, D_FF)), const((D_FF, D)), const((1, D))],
        out_specs=row_spec,
        out_shape=jax.ShapeDtypeStruct((B, S, D), F32),
        compiler_params=pltpu.CompilerParams(
            dimension_semantics=("arbitrary", "arbitrary"),
            vmem_limit_bytes=VMEM_LIMIT_FFN),
        name="outproj_ffn",
    )(x, o_moba, o_dil, moba_out_norm.reshape(1, half), dil_out_norm.reshape(1, half),
      w_out, ffn_norm.reshape(1, D), w_gate, w_up, w_down, final_norm.reshape(1, D))


def _rope_tables(S):
    inv_freq = ROPE_THETA ** (-jnp.arange(0, HEAD_DIM, 2, dtype=F32) / HEAD_DIM)
    ang = jnp.arange(S, dtype=F32)[:, None] * inv_freq[None, :]
    cos = jnp.concatenate([jnp.cos(ang)] * 4, axis=-1)
    sin = jnp.concatenate([-jnp.sin(ang), jnp.sin(ang)] * 2, axis=-1)
    return cos, sin


def kernel(x, attn_norm, w_in, moba_out_norm, dil_out_norm, w_out, ffn_norm, w_gate, w_up,
           w_down, final_norm):
    B, S, D = x.shape
    assert D == D_MODEL and S % TOKEN_TILE == 0 and S % (16 * WINDOW_LEN) == 0
    assert S // MOBA_BLOCK <= LANES - ONEHOT_LANE0
    cos, sin = _rope_tables(S)
    v0 = 2 * D_MODEL
    wv_moba_t = w_in[:, v0:v0 + N_HEADS_MOBA * HEAD_DIM].T.astype(BF16)
    qm, km, vmt, qd, kd, vd = _qkv_call(x, attn_norm, w_in.astype(BF16), wv_moba_t, cos, sin)
    o_moba = _moba_call(qm, km, vmt)
    o_dil = _dilated_call(qd, kd, vd)
    return _ffn_call(x, o_moba, o_dil, moba_out_norm, dil_out_norm, w_out.astype(BF16),
                     ffn_norm, w_gate.astype(BF16), w_up.astype(BF16), w_down.astype(BF16),
                     final_norm)
```

```python
import functools
import math

import jax
import jax.numpy as jnp
from jax import lax
from jax.experimental import pallas as pl
from jax.experimental.pallas import tpu as pltpu

F32 = jnp.float32
BF16 = jnp.bfloat16

D_MODEL = 1024
HEAD_DIM = 64
N_HEADS = D_MODEL // HEAD_DIM
N_HEADS_MOBA = N_HEADS // 2
N_HEADS_DIL = N_HEADS - N_HEADS_MOBA
MOBA_BLOCK = 256
MOBA_TOPK = 3
DILATIONS = (1, 4, 16)
WINDOW_LEN = 128
D_FF = 2816
ROPE_THETA = 10000.0
RMS_EPS = 1e-6
NEG_INF = -1e30
ATTN_SCALE = 1.0 / math.sqrt(HEAD_DIM)
Q_SCALE = ATTN_SCALE * math.log2(math.e)

LANES = 128
SUM_LANE = HEAD_DIM
VT_ROWS = HEAD_DIM + 16
ONEHOT_LANE0 = HEAD_DIM

TOKEN_TILE = 512
FFN_COL_CHUNKS = ((0, 1536), (1536, D_FF))
FFN_ROW_CHUNKS = 4
FFN_TAIL_PIECES = 2
MERGE_ROWS = 256
DILATED_GROUP = 32
MOBA_STEP = 2
GATE_CHUNK = 1024
VMEM_LIMIT_SMALL = 40 * 1024 * 1024
VMEM_LIMIT_FFN = 56 * 1024 * 1024

_NT = (((1,), (1,)), ((), ()))
_TN = (((0,), (0,)), ((), ()))


def _rms(y, gain):
    return y * lax.rsqrt(jnp.mean(y * y, axis=-1, keepdims=True) + RMS_EPS) * gain


def _qkv_kernel(x_ref, g_ref, w_ref, wvt_ref, cos_ref, sin_ref,
                qm_ref, km_ref, vmt_ref, qd_ref, kd_ref, vd_ref):
    tile = x_ref.shape[1]
    moba_width = N_HEADS_MOBA * HEAD_DIM
    h = _rms(x_ref[0], g_ref[...]).astype(BF16)
    cos = cos_ref[...]
    sin = sin_ref[...]
    lane = lax.broadcasted_iota(jnp.int32, (tile, LANES), 1)
    row = lax.broadcasted_iota(jnp.int32, (tile, LANES), 0)
    low_head = lane < HEAD_DIM
    first_half = (lane & (HEAD_DIM - 1)) < HEAD_DIM // 2
    block_of_row = (pl.program_id(1) * tile + row) // MOBA_BLOCK
    onehot = jnp.where(lane - ONEHOT_LANE0 == block_of_row, 1.0, 0.0)
    ones_col = jnp.where(lane == SUM_LANE, 1.0, 0.0)
    zeros = jnp.zeros((tile, LANES), F32)

    def rope(y):
        rot = jnp.where(first_half, pltpu.roll(y, LANES - HEAD_DIM // 2, 1),
                        pltpu.roll(y, HEAD_DIM // 2, 1))
        return y * cos + rot * sin

    def split(y, pad):
        return (jnp.where(low_head, y, pad),
                jnp.where(low_head, pltpu.roll(y, HEAD_DIM, 1), pad))

    q = jnp.dot(h, w_ref[:, 0:D_MODEL], preferred_element_type=F32)
    k = jnp.dot(h, w_ref[:, D_MODEL:2 * D_MODEL], preferred_element_type=F32)
    v = jnp.dot(h, w_ref[:, 2 * D_MODEL + moba_width:3 * D_MODEL], preferred_element_type=F32)
    for pair in range(N_HEADS // 2):
        cols = slice(pair * LANES, (pair + 1) * LANES)
        moba = pair < N_HEADS_MOBA // 2
        q_ref, k_ref = (qm_ref, km_ref) if moba else (qd_ref, kd_ref)
        head0 = 2 * pair - (0 if moba else N_HEADS_MOBA)
        qs = split(rope(q[:, cols]) * Q_SCALE, zeros)
        ks = split(rope(k[:, cols]), onehot if moba else zeros)
        for a in range(2):
            q_ref[0, head0 + a] = qs[a].astype(BF16)
            k_ref[0, head0 + a] = ks[a].astype(BF16)
        if not moba:
            vs = split(v[:, head0 * HEAD_DIM:head0 * HEAD_DIM + LANES], ones_col)
            for a in range(2):
                vd_ref[0, head0 + a] = vs[a].astype(BF16)

    vt = lax.dot_general(wvt_ref[...], h, _NT, preferred_element_type=F32)
    sub = lax.broadcasted_iota(jnp.int32, (VT_ROWS - HEAD_DIM, tile), 0)
    ones_row = jnp.where(sub == SUM_LANE - HEAD_DIM, 1.0, 0.0)
    for head in range(N_HEADS_MOBA):
        slab = jnp.concatenate([vt[head * HEAD_DIM:(head + 1) * HEAD_DIM, :], ones_row],
                               axis=0).astype(BF16)
        for c in range(tile // MOBA_BLOCK):
            vmt_ref[0, head, c] = slab[:, c * MOBA_BLOCK:(c + 1) * MOBA_BLOCK]


def _qkv_call(x, attn_norm, w_in, wv_moba_t, cos, sin):
    B, S, D = x.shape
    tile = TOKEN_TILE
    H = N_HEADS_MOBA
    head_shape = jax.ShapeDtypeStruct((B, H, S, LANES), BF16)
    head_spec = pl.BlockSpec((1, H, tile, LANES), lambda b, s: (b, 0, s, 0))
    vt_shape = jax.ShapeDtypeStruct((B, H, S // MOBA_BLOCK, VT_ROWS, MOBA_BLOCK), BF16)
    vt_spec = pl.BlockSpec((1, H, tile // MOBA_BLOCK, VT_ROWS, MOBA_BLOCK),
                           lambda b, s: (b, 0, s, 0, 0))
    return pl.pallas_call(
        _qkv_kernel,
        grid=(B, S // tile),
        in_specs=[
            pl.BlockSpec((1, tile, D), lambda b, s: (b, s, 0)),
            pl.BlockSpec((1, D), lambda b, s: (0, 0)),
            pl.BlockSpec((D, 3 * D), lambda b, s: (0, 0), pipeline_mode=pl.Buffered(1)),
            pl.BlockSpec(wv_moba_t.shape, lambda b, s: (0, 0), pipeline_mode=pl.Buffered(1)),
            pl.BlockSpec((tile, LANES), lambda b, s: (s, 0)),
            pl.BlockSpec((tile, LANES), lambda b, s: (s, 0)),
        ],
        out_specs=[head_spec, head_spec, vt_spec, head_spec, head_spec, head_spec],
        out_shape=[head_shape, head_shape, vt_shape, head_shape, head_shape, head_shape],
        compiler_params=pltpu.CompilerParams(
            dimension_semantics=("arbitrary", "arbitrary"),
            vmem_limit_bytes=VMEM_LIMIT_SMALL),
        name="qkv_rope",
    )(x, attn_norm.reshape(1, D), w_in, wv_moba_t, cos, sin)


def _moba_kernel(q_ref, k_ref, vt_ref, o_ref, qb_ref, s_ref, max_ref):
    blk = MOBA_BLOCK
    S = k_ref.shape[2]
    n_blocks = S // blk

    kmean = jnp.concatenate(
        [jnp.mean(k_ref[0, 0, n * blk:(n + 1) * blk, :].astype(F32), axis=0, keepdims=True)
         for n in range(n_blocks)], axis=0)
    sub = lax.broadcasted_iota(jnp.int32, (n_blocks, LANES), 0)
    lane = lax.broadcasted_iota(jnp.int32, (n_blocks, LANES), 1)
    place = jnp.where(lane == sub + ONEHOT_LANE0, 1.0, 0.0).astype(BF16)
    k_hi = kmean.astype(BF16)
    k_mid = (kmean - k_hi.astype(F32)).astype(BF16)
    k_lo = (kmean - k_hi.astype(F32) - k_mid.astype(F32)).astype(BF16)
    k_terms = jnp.concatenate([k_hi, k_mid, k_lo], axis=0)
    gc = GATE_CHUNK
    n_idx = lax.broadcasted_iota(jnp.int32, (n_blocks, gc), 0)
    pos = lax.broadcasted_iota(jnp.int32, (n_blocks, gc), 1)
    q_lanes = lax.broadcasted_iota(jnp.int32, (gc, LANES), 1) < HEAD_DIM
    for c in range(S // gc):
        q = q_ref[0, 0, c * gc:(c + 1) * gc, :]
        g3 = lax.dot_general(k_terms, q, _NT, preferred_element_type=F32)
        gate_t = g3[:n_blocks] + g3[n_blocks:2 * n_blocks] + g3[2 * n_blocks:]
        own = (pos + c * gc) // blk
        past = n_idx < own
        gate_t = jnp.where(past, gate_t, NEG_INF)
        chosen = jnp.zeros((n_blocks, gc), F32)
        for _ in range(MOBA_TOPK):
            best = jnp.max(gate_t, axis=0, keepdims=True)
            first = jnp.min(jnp.where(gate_t == best, n_idx, n_blocks), axis=0, keepdims=True)
            hit = n_idx == first
            chosen = jnp.where(hit, 1.0, chosen)
            gate_t = jnp.where(hit, -jnp.inf, gate_t)
        visible = ((chosen > 0.0) & past) | (n_idx == own)
        bias_t = jnp.where(visible, 0.0, NEG_INF).astype(BF16)
        bias = lax.dot_general(bias_t, place, _TN, preferred_element_type=F32)
        qb_ref[c * gc:(c + 1) * gc, :] = jnp.where(q_lanes, q, bias.astype(BF16))

    def score_pass(own):
        qb = qb_ref[own * blk:(own + 1) * blk, :]
        m8 = None
        for j in range(own + 1):
            s = lax.dot_general(k_ref[0, 0, j * blk:(j + 1) * blk, :], qb, _NT,
                                preferred_element_type=F32)
            if j == own:
                key_idx = lax.broadcasted_iota(jnp.int32, (blk, blk), 0)
                qry_idx = lax.broadcasted_iota(jnp.int32, (blk, blk), 1)
                s = jnp.where(key_idx <= qry_idx, s, NEG_INF)
            s_ref[own % (2 * MOBA_STEP), j * blk:(j + 1) * blk, :] = s
            s8 = jnp.max(s.reshape(blk // 8, 8, blk), axis=0)
            m8 = s8 if m8 is None else jnp.maximum(m8, s8)
            yield
        max_ref[own:own + 1, :] = jnp.max(m8, axis=0, keepdims=True)

    def value_pass(own):
        m = max_ref[own:own + 1, :]
        acc_t = None
        for j in range(own + 1):
            p = jnp.exp2(s_ref[own % (2 * MOBA_STEP), j * blk:(j + 1) * blk, :] - m).astype(BF16)
            part = jnp.dot(vt_ref[0, 0, j], p, preferred_element_type=F32)
            acc_t = part if acc_t is None else acc_t + part
            yield
        out_t = acc_t / acc_t[SUM_LANE:SUM_LANE + 1, :]
        out_t = jnp.concatenate([out_t, jnp.zeros((LANES - VT_ROWS, blk), F32)], axis=0)
        o_ref[0, 0, own * blk:(own + 1) * blk, :] = out_t.T.astype(BF16)

    def interleave(*steps):
        steps = list(steps)
        while steps:
            for g in list(steps):
                if next(g, StopIteration) is StopIteration:
                    steps.remove(g)

    n_steps = n_blocks // MOBA_STEP
    for t in range(n_steps + 1):
        @pl.when(pl.program_id(0) > -1 - t)
        def _(t=t):
            parts = []
            for i in range(MOBA_STEP):
                if t > 0:
                    parts.append(value_pass((t - 1) * MOBA_STEP + i))
                if t < n_steps:
                    parts.append(score_pass(t * MOBA_STEP + i))
            interleave(*parts)


def _moba_call(q, k, vt):
    B, H, S, _ = q.shape
    blk = MOBA_BLOCK
    assert S % GATE_CHUNK == 0
    seq_spec = pl.BlockSpec((1, 1, S, LANES), lambda b, h: (b, h, 0, 0))
    vt_spec = pl.BlockSpec((1, 1, S // blk, VT_ROWS, blk), lambda b, h: (b, h, 0, 0, 0))
    return pl.pallas_call(
        _moba_kernel,
        grid=(B, H),
        in_specs=[seq_spec, seq_spec, vt_spec],
        out_specs=seq_spec,
        out_shape=jax.ShapeDtypeStruct((B, H, S, LANES), BF16),
        scratch_shapes=[pltpu.VMEM((S, LANES), BF16),
                        pltpu.VMEM((2 * MOBA_STEP, S, blk), F32),
                        pltpu.VMEM((S // blk, blk), F32)],
        compiler_params=pltpu.CompilerParams(
            dimension_semantics=("arbitrary", "arbitrary"),
            vmem_limit_bytes=VMEM_LIMIT_SMALL),
        name="moba_attention",
    )(q, k, vt)


def _dilated_kernel(q_ref, k_ref, v_ref, o_ref, stage_ref, stage4_ref,
                    q4_ref, k4_ref, v4_ref, q16_ref, k16_ref, v16_ref,
                    acc1_ref, acc4_ref, acc16_ref, max1_ref, max4_ref, max16_ref,
                    s_ref, m_ref):
    S = q_ref.shape[2]
    L = WINDOW_LEN
    q1_ref, k1_ref, v1_ref = q_ref.at[0, 0], k_ref.at[0, 0], v_ref.at[0, 0]

    n4, n16 = S // 4, S // 16

    def deinterleave(src, dst4, dst16):
        stage_ref[...] = src[...].astype(F32)
        for r in range(4):
            x = stage_ref[pl.ds(r, n4, stride=4), :]
            dst4[r * n4:(r + 1) * n4, :] = x.astype(BF16)
            stage4_ref[r * n4:(r + 1) * n4, :] = x
        for r in range(16):
            x = stage4_ref[pl.ds((r % 4) * n4 + r // 4, n16, stride=4), :]
            dst16[r * n16:(r + 1) * n16, :] = x.astype(BF16)
        yield

    acc16t_ref, max16t_ref = stage_ref, stage4_ref

    def order16():
        for r in range(4):
            acc16_ref[pl.ds(r, n4, stride=4), :] = acc16t_ref[r * n4:(r + 1) * n4, :]
            max16_ref[pl.ds(r, n4, stride=4), :] = max16t_ref[r * n4:(r + 1) * n4, :]
        yield

    tiles = []
    for d, qd_ref, kd_ref, vd_ref, acc_ref, max_ref in (
            (1, q1_ref, k1_ref, v1_ref, acc1_ref, max1_ref),
            (4, q4_ref, k4_ref, v4_ref, acc4_ref, max4_ref),
            (16, q16_ref, k16_ref, v16_ref, acc16t_ref, max16t_ref)):
        n = S // d
        for residue in range(d):
            for t in range(n // L):
                base = residue * n + t * L
                keys = slice(base - L if t > 0 else base, base + L)
                if d == 1:
                    rows = slice(base, base + L)
                elif d == 4:
                    rows = pl.ds(t * L * 4 + residue, L, stride=4)
                else:
                    rows = pl.ds((residue % 4) * n4 + t * L * 4 + residue // 4, L, stride=4)
                tiles.append((qd_ref, kd_ref, vd_ref, acc_ref, max_ref, base, keys, rows))
    group = DILATED_GROUP
    n_groups = len(tiles) // group

    def score_pass(g):
        for i, (qd_ref, kd_ref, _, _, max_ref, base, keys, rows) in enumerate(
                tiles[g * group:(g + 1) * group]):
            width = keys.stop - keys.start
            s = lax.dot_general(qd_ref[base:base + L, :], kd_ref[keys, :], _NT,
                                preferred_element_type=F32)
            r_idx = lax.broadcasted_iota(jnp.int32, (L, width), 0)
            c_idx = lax.broadcasted_iota(jnp.int32, (L, width), 1)
            if width == 2 * L:
                seen = ((c_idx < L) & (c_idx >= r_idx)) | ((c_idx >= L) & (c_idx - L <= r_idx))
            else:
                seen = c_idx <= r_idx
            s = jnp.where(seen, s, NEG_INF)
            m = jnp.max(s, axis=1, keepdims=True)
            m = jnp.broadcast_to(m, (L, LANES))
            s_ref[g % 2, i * L:(i + 1) * L, 0:width] = s
            m_ref[g % 2, i * L:(i + 1) * L, :] = m
            max_ref[rows, :] = m
            yield

    def value_pass(g):
        for i, (_, _, vd_ref, acc_ref, _, _, keys, rows) in enumerate(
                tiles[g * group:(g + 1) * group]):
            width = keys.stop - keys.start
            m = m_ref[g % 2, i * L:(i + 1) * L, :]
            p = jnp.concatenate(
                [jnp.exp2(s_ref[g % 2, i * L:(i + 1) * L, c * L:(c + 1) * L] - m)
                 for c in range(width // L)], axis=1).astype(BF16)
            acc_ref[rows, :] = jnp.dot(p, vd_ref[keys, :], preferred_element_type=F32)
            yield

    def merge():
        for c in range(S // MERGE_ROWS):
            rows = slice(c * MERGE_ROWS, (c + 1) * MERGE_ROWS)
            maxes = [ref[rows, :] for ref in (max1_ref, max4_ref, max16_ref)]
            m = jnp.maximum(jnp.maximum(maxes[0], maxes[1]), maxes[2])
            total = sum(jnp.exp2(mi - m) * ref[rows, :]
                        for mi, ref in zip(maxes, (acc1_ref, acc4_ref, acc16_ref)))
            o_ref[0, 0, rows, :] = (total / total[:, SUM_LANE:SUM_LANE + 1]).astype(BF16)
            yield

    def interleave(*steps):
        steps = list(steps)
        while steps:
            for g in list(steps):
                if next(g, StopIteration) is StopIteration:
                    steps.remove(g)

    copies = [(q1_ref, q4_ref, q16_ref), (k1_ref, k4_ref, k16_ref), (v1_ref, v4_ref, v16_ref)]
    first_dilated = (S // L) // group
    assert first_dilated >= 1
    copy_step = [0, min(1, first_dilated - 1), min(2, first_dilated)]
    for t in range(n_groups + 2):
        @pl.when(pl.program_id(0) > -1 - t)
        def _(t=t):
            parts = []
            if 0 < t <= n_groups:
                parts.append(value_pass(t - 1))
            if t < n_groups:
                parts.append(score_pass(t))
            parts.extend(deinterleave(*c) for c, at in zip(copies, copy_step) if at == t)
            if t == n_groups + 1:
                interleave(order16())
                parts.append(merge())
            interleave(*parts)


def _dilated_call(q, k, v):
    B, H, S, _ = q.shape
    seq_spec = pl.BlockSpec((1, 1, S, LANES), lambda b, h: (b, h, 0, 0))
    return pl.pallas_call(
        _dilated_kernel,
        grid=(B, H),
        in_specs=[seq_spec] * 3,
        out_specs=seq_spec,
        out_shape=jax.ShapeDtypeStruct((B, H, S, LANES), BF16),
        scratch_shapes=([pltpu.VMEM((S, LANES), F32)] * 2
                        + [pltpu.VMEM((S, LANES), BF16)] * 6
                        + [pltpu.VMEM((S, LANES), F32)] * 6
                        + [pltpu.VMEM((2, DILATED_GROUP * WINDOW_LEN, 2 * WINDOW_LEN), F32),
                           pltpu.VMEM((2, DILATED_GROUP * WINDOW_LEN, LANES), F32)]),
        compiler_params=pltpu.CompilerParams(
            dimension_semantics=("arbitrary", "arbitrary"),
            vmem_limit_bytes=VMEM_LIMIT_FFN),
        name="dilated_attention",
    )(q, k, v)


def _ffn_kernel(x_ref, om_ref, od_ref, gm_ref, gd_ref, wo_ref, gf_ref,
                wg_ref, wu_ref, wd_ref, gl_ref, out_ref):
    tile = x_ref.shape[1]
    sub = tile // FFN_ROW_CHUNKS
    lane = lax.broadcasted_iota(jnp.int32, (sub, LANES), 1)
    low_head = lane < HEAD_DIM

    def heads(ref, rows):
        pairs = []
        for p in range(ref.shape[1] // 2):
            a = ref[0, 2 * p, rows, :].astype(F32)
            b = ref[0, 2 * p + 1, rows, :].astype(F32)
            pairs.append(jnp.where(low_head, a, pltpu.roll(b, HEAD_DIM, 1)))
        return jnp.concatenate(pairs, axis=1)

    x1_parts, h_parts = [], []
    for r in range(FFN_ROW_CHUNKS):
        rows = slice(r * sub, (r + 1) * sub)
        mixed = jnp.concatenate([_rms(heads(om_ref, rows), gm_ref[...]),
                                 _rms(heads(od_ref, rows), gd_ref[...])], axis=1).astype(BF16)
        x1 = x_ref[0, rows, :] + jnp.dot(mixed, wo_ref[...], preferred_element_type=F32)
        x1_parts.append(x1)
        h_parts.append(_rms(x1, gf_ref[...]).astype(BF16))
    x2 = jnp.concatenate(x1_parts, axis=0)
    h = jnp.concatenate(h_parts, axis=0)
    for lo, hi in FFN_COL_CHUNKS[:-1]:
        g = jnp.dot(h, wg_ref[:, lo:hi], preferred_element_type=F32)
        u = jnp.dot(h, wu_ref[:, lo:hi], preferred_element_type=F32)
        act = (g * jax.nn.sigmoid(g) * u).astype(BF16)
        x2 = x2 + jnp.dot(act, wd_ref[lo:hi, :], preferred_element_type=F32)
    lo, hi = FFN_COL_CHUNKS[-1]
    g = jnp.dot(h, wg_ref[:, lo:hi], preferred_element_type=F32)
    u = jnp.dot(h, wu_ref[:, lo:hi], preferred_element_type=F32)
    act = (g * jax.nn.sigmoid(g) * u).astype(BF16)
    piece = tile // FFN_TAIL_PIECES
    for r in range(FFN_TAIL_PIECES):
        rows = slice(r * piece, (r + 1) * piece)
        y = x2[rows, :] + jnp.dot(act[rows, :], wd_ref[lo:hi, :], preferred_element_type=F32)
        out_ref[0, rows, :] = _rms(y, gl_ref[...])


def _ffn_call(x, o_moba, o_dil, moba_out_norm, dil_out_norm, w_out, ffn_norm,
              w_gate, w_up, w_down, final_norm):
    B, S, D = x.shape
    tile = TOKEN_TILE
    half = N_HEADS_MOBA * HEAD_DIM

    def const(shape):
        return pl.BlockSpec(shape, lambda b, s: (0,) * len(shape), pipeline_mode=pl.Buffered(1))

    row_spec = pl.BlockSpec((1, tile, D), lambda b, s: (b, s, 0))
    head_spec = pl.BlockSpec((1, N_HEADS_MOBA, tile, LANES), lambda b, s: (b, 0, s, 0))
    return pl.pallas_call(
        _ffn_kernel,
        grid=(B, S // tile),
        in_specs=[row_spec, head_spec, head_spec,
                  const((1, half)), const((1, half)), const((D, D)), const((1, D)),
                  const((D, D_FF)), const((D, D_FF)), const((D_FF, D)), const((1, D))],
        out_specs=row_spec,
        out_shape=jax.ShapeDtypeStruct((B, S, D), F32),
        compiler_params=pltpu.CompilerParams(
            dimension_semantics=("arbitrary", "arbitrary"),
            vmem_limit_bytes=VMEM_LIMIT_FFN),
        name="outproj_ffn",
    )(x, o_moba, o_dil, moba_out_norm.reshape(1, half), dil_out_norm.reshape(1, half),
      w_out, ffn_norm.reshape(1, D), w_gate, w_up, w_down, final_norm.reshape(1, D))


def _rope_tables(S):
    inv_freq = ROPE_THETA ** (-jnp.arange(0, HEAD_DIM, 2, dtype=F32) / HEAD_DIM)
    ang = jnp.arange(S, dtype=F32)[:, None] * inv_freq[None, :]
    cos = jnp.concatenate([jnp.cos(ang)] * 4, axis=-1)
    sin = jnp.concatenate([-jnp.sin(ang), jnp.sin(ang)] * 2, axis=-1)
    return cos, sin


def kernel(x, attn_norm, w_in, moba_out_norm, dil_out_norm, w_out, ffn_norm, w_gate, w_up,
           w_down, final_norm):
    B, S, D = x.shape
    assert D == D_MODEL and S % TOKEN_TILE == 0 and S % (16 * WINDOW_LEN) == 0
    assert S // MOBA_BLOCK <= LANES - ONEHOT_LANE0
    cos, sin = _rope_tables(S)
    v0 = 2 * D_MODEL
    wv_moba_t = w_in[:, v0:v0 + N_HEADS_MOBA * HEAD_DIM].T.astype(BF16)
    qm, km, vmt, qd, kd, vd = _qkv_call(x, attn_norm, w_in.astype(BF16), wv_moba_t, cos, sin)
    o_moba = _moba_call(qm, km, vmt)
    o_dil = _dilated_call(qd, kd, vd)
    return _ffn_call(x, o_moba, o_dil, moba_out_norm, dil_out_norm, w_out.astype(BF16),
                     ffn_norm, w_gate.astype(BF16), w_up.astype(BF16), w_down.astype(BF16),
                     final_norm)
```

```python
import math

import jax
import jax.numpy as jnp
import numpy as np
from jax import lax
from jax.experimental import pallas as pl
from jax.experimental.pallas import tpu as pltpu

F32 = jnp.float32
BF16 = jnp.bfloat16

D_MODEL = 1024
HEAD_DIM = 64
N_HEADS = D_MODEL // HEAD_DIM
N_HEADS_MOBA = N_HEADS // 2
MOBA_BLOCK = 256
MOBA_TOPK = 3
DILATIONS = (1, 4, 16)
WINDOW_LEN = 128
D_FF = 2816
ROPE_THETA = 10000.0
RMS_EPS = 1e-6
NEG_INF = -1e30
ATTN_SCALE = 1.0 / math.sqrt(HEAD_DIM)
Q_SCALE = ATTN_SCALE * math.log2(math.e)

LANES = 128
F32_SUBLANES = 8
BF16_SUBLANES = 16
SUM_LANE = HEAD_DIM
VT_ROWS = HEAD_DIM + BF16_SUBLANES
ONEHOT_LANE0 = HEAD_DIM

TOKEN_TILE = 512
FFN_COL_CHUNKS = ((0, 1536), (1536, D_FF))
FFN_ROW_CHUNKS = 4
FFN_TAIL_PIECES = 2
MERGE_ROWS = 256
DILATED_GROUP = 48
MOBA_STEP = 2
GATE_CHUNK = 1024
VMEM_LIMIT_SMALL = 40 * 1024 * 1024
VMEM_LIMIT_LARGE = 56 * 1024 * 1024

_NT = (((1,), (1,)), ((), ()))
_TN = (((0,), (0,)), ((), ()))


def _interleave(*generators):
    running = list(generators)
    while running:
        for g in list(running):
            if next(g, StopIteration) is StopIteration:
                running.remove(g)


def _rms(y, gain):
    return y * lax.rsqrt(jnp.mean(y * y, axis=-1, keepdims=True) + RMS_EPS) * gain


def _qkv_kernel(x_ref, g_ref, w_ref, wvt_ref, cos_ref, sin_ref,
                qm_ref, km_ref, vmt_ref, qd_ref, kd_ref, vd_ref):
    tile = x_ref.shape[1]
    moba_width = N_HEADS_MOBA * HEAD_DIM
    h = _rms(x_ref[0], g_ref[...]).astype(BF16)
    cos = cos_ref[...]
    sin = sin_ref[...]
    lane = lax.broadcasted_iota(jnp.int32, (tile, LANES), 1)
    row = lax.broadcasted_iota(jnp.int32, (tile, LANES), 0)
    low_head = lane < HEAD_DIM
    first_half = (lane & (HEAD_DIM - 1)) < HEAD_DIM // 2
    block_of_row = (pl.program_id(1) * tile + row) // MOBA_BLOCK
    onehot = jnp.where(lane - ONEHOT_LANE0 == block_of_row, 1.0, 0.0)
    ones_col = jnp.where(lane == SUM_LANE, 1.0, 0.0)
    zeros = jnp.zeros((tile, LANES), F32)

    def rope(y):
        rot = jnp.where(first_half, pltpu.roll(y, LANES - HEAD_DIM // 2, 1),
                        pltpu.roll(y, HEAD_DIM // 2, 1))
        return y * cos + rot * sin

    def split(y, pad):
        return (jnp.where(low_head, y, pad),
                jnp.where(low_head, pltpu.roll(y, HEAD_DIM, 1), pad))

    q = jnp.dot(h, w_ref[:, 0:D_MODEL], preferred_element_type=F32)
    k = jnp.dot(h, w_ref[:, D_MODEL:2 * D_MODEL], preferred_element_type=F32)
    v = jnp.dot(h, w_ref[:, 2 * D_MODEL + moba_width:3 * D_MODEL], preferred_element_type=F32)
    for pair in range(N_HEADS // 2):
        cols = slice(pair * LANES, (pair + 1) * LANES)
        moba = pair < N_HEADS_MOBA // 2
        q_ref, k_ref = (qm_ref, km_ref) if moba else (qd_ref, kd_ref)
        head0 = 2 * pair - (0 if moba else N_HEADS_MOBA)
        qs = split(rope(q[:, cols]) * Q_SCALE, zeros)
        ks = split(rope(k[:, cols]), onehot if moba else zeros)
        for a in range(2):
            q_ref[0, head0 + a] = qs[a].astype(BF16)
            k_ref[0, head0 + a] = ks[a].astype(BF16)
        if not moba:
            vs = split(v[:, head0 * HEAD_DIM:head0 * HEAD_DIM + LANES], ones_col)
            for a in range(2):
                vd_ref[0, head0 + a] = vs[a].astype(BF16)

    vt = lax.dot_general(wvt_ref[...], h, _NT, preferred_element_type=F32)
    sub = lax.broadcasted_iota(jnp.int32, (VT_ROWS - HEAD_DIM, tile), 0)
    ones_row = jnp.where(sub == SUM_LANE - HEAD_DIM, 1.0, 0.0)
    for head in range(N_HEADS_MOBA):
        slab = jnp.concatenate([vt[head * HEAD_DIM:(head + 1) * HEAD_DIM, :], ones_row],
                               axis=0).astype(BF16)
        for c in range(tile // MOBA_BLOCK):
            vmt_ref[0, head, c] = slab[:, c * MOBA_BLOCK:(c + 1) * MOBA_BLOCK]


def _qkv_call(x, attn_norm, w_in, wv_moba_t, cos, sin):
    B, S, D = x.shape
    tile = TOKEN_TILE
    H = N_HEADS_MOBA
    head_shape = jax.ShapeDtypeStruct((B, H, S, LANES), BF16)
    head_spec = pl.BlockSpec((1, H, tile, LANES), lambda b, s: (b, 0, s, 0))
    vt_shape = jax.ShapeDtypeStruct((B, H, S // MOBA_BLOCK, VT_ROWS, MOBA_BLOCK), BF16)
    vt_spec = pl.BlockSpec((1, H, tile // MOBA_BLOCK, VT_ROWS, MOBA_BLOCK),
                           lambda b, s: (b, 0, s, 0, 0))
    return pl.pallas_call(
        _qkv_kernel,
        grid=(B, S // tile),
        in_specs=[
            pl.BlockSpec((1, tile, D), lambda b, s: (b, s, 0)),
            pl.BlockSpec((1, D), lambda b, s: (0, 0)),
            pl.BlockSpec((D, 3 * D), lambda b, s: (0, 0), pipeline_mode=pl.Buffered(1)),
            pl.BlockSpec(wv_moba_t.shape, lambda b, s: (0, 0), pipeline_mode=pl.Buffered(1)),
            pl.BlockSpec((tile, LANES), lambda b, s: (s, 0)),
            pl.BlockSpec((tile, LANES), lambda b, s: (s, 0)),
        ],
        out_specs=[head_spec, head_spec, vt_spec, head_spec, head_spec, head_spec],
        out_shape=[head_shape, head_shape, vt_shape, head_shape, head_shape, head_shape],
        compiler_params=pltpu.CompilerParams(
            dimension_semantics=("arbitrary", "arbitrary"),
            vmem_limit_bytes=VMEM_LIMIT_SMALL),
        name="qkv_rope",
    )(x, attn_norm.reshape(1, D), w_in, wv_moba_t, cos, sin)


def _moba_kernel(q_ref, k_ref, vt_ref, o_ref, qb_ref, s_ref, max_ref):
    blk = MOBA_BLOCK
    S = k_ref.shape[2]
    n_blocks = S // blk

    kmean = jnp.concatenate(
        [jnp.mean(k_ref[0, 0, n * blk:(n + 1) * blk, :].astype(F32), axis=0, keepdims=True)
         for n in range(n_blocks)], axis=0)
    sub = lax.broadcasted_iota(jnp.int32, (n_blocks, LANES), 0)
    lane = lax.broadcasted_iota(jnp.int32, (n_blocks, LANES), 1)
    place = jnp.where(lane == sub + ONEHOT_LANE0, 1.0, 0.0).astype(BF16)
    k_hi = kmean.astype(BF16)
    k_mid = (kmean - k_hi.astype(F32)).astype(BF16)
    k_lo = (kmean - k_hi.astype(F32) - k_mid.astype(F32)).astype(BF16)
    k_terms = jnp.concatenate([k_hi, k_mid, k_lo], axis=0)
    gc = GATE_CHUNK
    n_idx = lax.broadcasted_iota(jnp.int32, (n_blocks, gc), 0)
    pos = lax.broadcasted_iota(jnp.int32, (n_blocks, gc), 1)
    q_lanes = lax.broadcasted_iota(jnp.int32, (gc, LANES), 1) < HEAD_DIM
    for c in range(S // gc):
        q = q_ref[0, 0, c * gc:(c + 1) * gc, :]
        g3 = lax.dot_general(k_terms, q, _NT, preferred_element_type=F32)
        gate_t = g3[:n_blocks] + g3[n_blocks:2 * n_blocks] + g3[2 * n_blocks:]
        own = (pos + c * gc) // blk
        past = n_idx < own
        gate_t = jnp.where(past, gate_t, NEG_INF)
        chosen = jnp.zeros((n_blocks, gc), F32)
        for _ in range(MOBA_TOPK):
            best = jnp.max(gate_t, axis=0, keepdims=True)
            first = jnp.min(jnp.where(gate_t == best, n_idx, n_blocks), axis=0, keepdims=True)
            hit = n_idx == first
            chosen = jnp.where(hit, 1.0, chosen)
            gate_t = jnp.where(hit, -jnp.inf, gate_t)
        visible = ((chosen > 0.0) & past) | (n_idx == own)
        bias_t = jnp.where(visible, 0.0, NEG_INF).astype(BF16)
        bias = lax.dot_general(bias_t, place, _TN, preferred_element_type=F32)
        qb_ref[c * gc:(c + 1) * gc, :] = jnp.where(q_lanes, q, bias.astype(BF16))

    def score_pass(own):
        qb = qb_ref[own * blk:(own + 1) * blk, :]
        m8 = None
        for j in range(own + 1):
            s = lax.dot_general(k_ref[0, 0, j * blk:(j + 1) * blk, :], qb, _NT,
                                preferred_element_type=F32)
            if j == own:
                key_idx = lax.broadcasted_iota(jnp.int32, (blk, blk), 0)
                qry_idx = lax.broadcasted_iota(jnp.int32, (blk, blk), 1)
                s = jnp.where(key_idx <= qry_idx, s, NEG_INF)
            s_ref[own % (2 * MOBA_STEP), j * blk:(j + 1) * blk, :] = s
            s8 = jnp.max(s.reshape(blk // F32_SUBLANES, F32_SUBLANES, blk), axis=0)
            m8 = s8 if m8 is None else jnp.maximum(m8, s8)
            yield
        max_ref[own:own + 1, :] = jnp.max(m8, axis=0, keepdims=True)

    def value_pass(own):
        m = max_ref[own:own + 1, :]
        acc_t = None
        for j in range(own + 1):
            p = jnp.exp2(s_ref[own % (2 * MOBA_STEP), j * blk:(j + 1) * blk, :] - m).astype(BF16)
            part = jnp.dot(vt_ref[0, 0, j], p, preferred_element_type=F32)
            acc_t = part if acc_t is None else acc_t + part
            yield
        out_t = acc_t / acc_t[SUM_LANE:SUM_LANE + 1, :]
        out_t = jnp.concatenate([out_t, jnp.zeros((LANES - VT_ROWS, blk), F32)], axis=0)
        o_ref[0, 0, own * blk:(own + 1) * blk, :] = out_t.T.astype(BF16)

    n_steps = n_blocks // MOBA_STEP
    for t in range(n_steps + 1):
        @pl.when(pl.program_id(0) > -1 - t)
        def _(t=t):
            parts = []
            for i in range(MOBA_STEP):
                if t > 0:
                    parts.append(value_pass((t - 1) * MOBA_STEP + i))
                if t < n_steps:
                    parts.append(score_pass(t * MOBA_STEP + i))
            _interleave(*parts)


def _moba_call(q, k, vt):
    B, H, S, _ = q.shape
    blk = MOBA_BLOCK
    assert S % GATE_CHUNK == 0
    seq_spec = pl.BlockSpec((1, 1, S, LANES), lambda b, h: (b, h, 0, 0))
    vt_spec = pl.BlockSpec((1, 1, S // blk, VT_ROWS, blk), lambda b, h: (b, h, 0, 0, 0))
    return pl.pallas_call(
        _moba_kernel,
        grid=(B, H),
        in_specs=[seq_spec, seq_spec, vt_spec],
        out_specs=seq_spec,
        out_shape=jax.ShapeDtypeStruct((B, H, S, LANES), BF16),
        scratch_shapes=[pltpu.VMEM((S, LANES), BF16),
                        pltpu.VMEM((2 * MOBA_STEP, S, blk), F32),
                        pltpu.VMEM((S // blk, blk), F32)],
        compiler_params=pltpu.CompilerParams(
            dimension_semantics=("arbitrary", "arbitrary"),
            vmem_limit_bytes=VMEM_LIMIT_SMALL),
        name="moba_attention",
    )(q, k, vt)


def _dilated_kernel(q_ref, k_ref, v_ref, o_ref, stage_ref, stage4_ref,
                    q4_ref, k4_ref, v4_ref, q16_ref, k16_ref, v16_ref,
                    acc1_ref, acc4_ref, acc16_ref, max1_ref, max4_ref, max16_ref,
                    acc16t_ref, max16t_ref, s_ref):
    S = q_ref.shape[2]
    L = WINDOW_LEN
    q1_ref, k1_ref, v1_ref = q_ref.at[0, 0], k_ref.at[0, 0], v_ref.at[0, 0]

    n4, n16 = S // 4, S // 16

    def deinterleave(src, dst4, dst16):
        stage_ref[...] = src[...].astype(F32)
        for r in range(4):
            x = stage_ref[pl.ds(r, n4, stride=4), :]
            dst4[r * n4:(r + 1) * n4, :] = x.astype(BF16)
            stage4_ref[r * n4:(r + 1) * n4, :] = x
        for r in range(16):
            x = stage4_ref[pl.ds((r % 4) * n4 + r // 4, n16, stride=4), :]
            dst16[r * n16:(r + 1) * n16, :] = x.astype(BF16)
        yield

    def order16(half):
        h4 = n4 // 2
        for r in range(4):
            src = slice(r * n4 + half * h4, r * n4 + (half + 1) * h4)
            dst = pl.ds(half * (S // 2) + r, h4, stride=4)
            acc16_ref[dst, :] = acc16t_ref[src, :]
            max16_ref[dst, :] = max16t_ref[src, :]
        yield

    tiles = []
    for d, qd_ref, kd_ref, vd_ref, acc_ref, max_ref in (
            (1, q1_ref, k1_ref, v1_ref, acc1_ref, max1_ref),
            (4, q4_ref, k4_ref, v4_ref, acc4_ref, max4_ref),
            (16, q16_ref, k16_ref, v16_ref, acc16t_ref, max16t_ref)):
        n = S // d
        for residue in range(d):
            for t in range(n // L):
                base = residue * n + t * L
                keys = slice(base - L if t > 0 else base, base + L)
                if d == 1:
                    rows = slice(base, base + L)
                elif d == 4:
                    rows = pl.ds(t * L * 4 + residue, L, stride=4)
                else:
                    rows = pl.ds((residue % 4) * n4 + t * L * 4 + residue // 4, L, stride=4)
                half = (2 * t) // (n // L)
                tiles.append((half, qd_ref, kd_ref, vd_ref, acc_ref, max_ref, base, keys, rows))
    tiles = [tile[1:] for tile in sorted(tiles, key=lambda tile: tile[0])]
    group = DILATED_GROUP
    n_groups = len(tiles) // group

    def score_pass(g):
        for i, (qd_ref, kd_ref, _, _, max_ref, base, keys, rows) in enumerate(
                tiles[g * group:(g + 1) * group]):
            width = keys.stop - keys.start
            s = lax.dot_general(qd_ref[base:base + L, :], kd_ref[keys, :], _NT,
                                preferred_element_type=F32)
            r_idx = lax.broadcasted_iota(jnp.int32, (L, width), 0)
            c_idx = lax.broadcasted_iota(jnp.int32, (L, width), 1)
            if width == 2 * L:
                seen = ((c_idx < L) & (c_idx >= r_idx)) | ((c_idx >= L) & (c_idx - L <= r_idx))
            else:
                seen = c_idx <= r_idx
            s = jnp.where(seen, s, NEG_INF)
            m = jnp.max(s, axis=1, keepdims=True)
            m = jnp.broadcast_to(m, (L, LANES))
            s_ref[g % 2, i * L:(i + 1) * L, 0:width] = s
            max_ref[rows, :] = m
            yield

    def value_pass(g):
        for i, (_, _, vd_ref, acc_ref, max_ref, _, keys, rows) in enumerate(
                tiles[g * group:(g + 1) * group]):
            width = keys.stop - keys.start
            m = max_ref[rows, :]
            p = jnp.concatenate(
                [jnp.exp2(s_ref[g % 2, i * L:(i + 1) * L, c * L:(c + 1) * L] - m)
                 for c in range(width // L)], axis=1).astype(BF16)
            acc_ref[rows, :] = jnp.dot(p, vd_ref[keys, :], preferred_element_type=F32)
            yield

    def merge(half):
        chunks = S // MERGE_ROWS // 2
        for c in range(half * chunks, (half + 1) * chunks):
            rows = slice(c * MERGE_ROWS, (c + 1) * MERGE_ROWS)
            maxes = [ref[rows, :] for ref in (max1_ref, max4_ref, max16_ref)]
            m = jnp.maximum(jnp.maximum(maxes[0], maxes[1]), maxes[2])
            total = sum(jnp.exp2(mi - m) * ref[rows, :]
                        for mi, ref in zip(maxes, (acc1_ref, acc4_ref, acc16_ref)))
            o_ref[0, 0, rows, :] = (total / total[:, SUM_LANE:SUM_LANE + 1]).astype(BF16)
            yield

    copies = [(q1_ref, q4_ref, q16_ref), (k1_ref, k4_ref, k16_ref), (v1_ref, v4_ref, v16_ref)]
    assert n_groups % 2 == 0
    merge_step = {n_groups // 2 + 1: 0, n_groups + 1: 1}
    for t in range(n_groups + 2):
        @pl.when(pl.program_id(0) > -1 - t)
        def _(t=t):
            parts = []
            if t == 0:
                parts += [deinterleave(*copies[0]), deinterleave(*copies[1])]
            if t == 1:
                parts.append(deinterleave(*copies[2]))
            if 0 < t <= n_groups:
                parts.append(value_pass(t - 1))
            if t < n_groups:
                parts.append(score_pass(t))
            if t in merge_step:
                _interleave(order16(merge_step[t]))
                parts.append(merge(merge_step[t]))
            _interleave(*parts)


def _dilated_call(q, k, v):
    B, H, S, _ = q.shape
    seq_spec = pl.BlockSpec((1, 1, S, LANES), lambda b, h: (b, h, 0, 0))
    return pl.pallas_call(
        _dilated_kernel,
        grid=(B, H),
        in_specs=[seq_spec] * 3,
        out_specs=seq_spec,
        out_shape=jax.ShapeDtypeStruct((B, H, S, LANES), BF16),
        scratch_shapes=([pltpu.VMEM((S, LANES), F32)] * 2
                        + [pltpu.VMEM((S, LANES), BF16)] * 6
                        + [pltpu.VMEM((S, LANES), F32)] * 8
                        + [pltpu.VMEM((2, DILATED_GROUP * WINDOW_LEN, 2 * WINDOW_LEN), F32)]),
        compiler_params=pltpu.CompilerParams(
            dimension_semantics=("arbitrary", "arbitrary"),
            vmem_limit_bytes=VMEM_LIMIT_LARGE),
        name="dilated_attention",
    )(q, k, v)


def _ffn_kernel(x_ref, om_ref, od_ref, gm_ref, gd_ref, wo_ref, gf_ref,
                wg_ref, wu_ref, wd_ref, gl_ref, out_ref):
    tile = x_ref.shape[1]
    sub = tile // FFN_ROW_CHUNKS
    lane = lax.broadcasted_iota(jnp.int32, (sub, LANES), 1)
    low_head = lane < HEAD_DIM

    def heads(ref, rows):
        pairs = []
        for p in range(ref.shape[1] // 2):
            a = ref[0, 2 * p, rows, :].astype(F32)
            b = ref[0, 2 * p + 1, rows, :].astype(F32)
            pairs.append(jnp.where(low_head, a, pltpu.roll(b, HEAD_DIM, 1)))
        return jnp.concatenate(pairs, axis=1)

    x1_parts, h_parts = [], []
    for r in range(FFN_ROW_CHUNKS):
        rows = slice(r * sub, (r + 1) * sub)
        mixed = jnp.concatenate([_rms(heads(om_ref, rows), gm_ref[...]),
                                 _rms(heads(od_ref, rows), gd_ref[...])], axis=1).astype(BF16)
        x1 = x_ref[0, rows, :] + jnp.dot(mixed, wo_ref[...], preferred_element_type=F32)
        x1_parts.append(x1)
        h_parts.append(_rms(x1, gf_ref[...]).astype(BF16))
    x2 = jnp.concatenate(x1_parts, axis=0)
    h = jnp.concatenate(h_parts, axis=0)
    for lo, hi in FFN_COL_CHUNKS[:-1]:
        g = jnp.dot(h, wg_ref[:, lo:hi], preferred_element_type=F32)
        u = jnp.dot(h, wu_ref[:, lo:hi], preferred_element_type=F32)
        act = (g * jax.nn.sigmoid(g) * u).astype(BF16)
        x2 = x2 + jnp.dot(act, wd_ref[lo:hi, :], preferred_element_type=F32)
    lo, hi = FFN_COL_CHUNKS[-1]
    g = jnp.dot(h, wg_ref[:, lo:hi], preferred_element_type=F32)
    u = jnp.dot(h, wu_ref[:, lo:hi], preferred_element_type=F32)
    act = (g * jax.nn.sigmoid(g) * u).astype(BF16)
    piece = tile // FFN_TAIL_PIECES
    for r in range(FFN_TAIL_PIECES):
        rows = slice(r * piece, (r + 1) * piece)
        y = x2[rows, :] + jnp.dot(act[rows, :], wd_ref[lo:hi, :], preferred_element_type=F32)
        out_ref[0, rows, :] = _rms(y, gl_ref[...])


def _ffn_call(x, o_moba, o_dil, moba_out_norm, dil_out_norm, w_out, ffn_norm,
              w_gate, w_up, w_down, final_norm):
    B, S, D = x.shape
    tile = TOKEN_TILE
    half = N_HEADS_MOBA * HEAD_DIM

    def const(shape):
        return pl.BlockSpec(shape, lambda b, s: (0,) * len(shape), pipeline_mode=pl.Buffered(1))

    row_spec = pl.BlockSpec((1, tile, D), lambda b, s: (b, s, 0))
    head_spec = pl.BlockSpec((1, N_HEADS_MOBA, tile, LANES), lambda b, s: (b, 0, s, 0))
    return pl.pallas_call(
        _ffn_kernel,
        grid=(B, S // tile),
        in_specs=[row_spec, head_spec, head_spec,
                  const((1, half)), const((1, half)), const((D, D)), const((1, D)),
                  const((D, D_FF)), const((D, D_FF)), const((D_FF, D)), const((1, D))],
        out_specs=row_spec,
        out_shape=jax.ShapeDtypeStruct((B, S, D), F32),
        compiler_params=pltpu.CompilerParams(
            dimension_semantics=("arbitrary", "arbitrary"),
            vmem_limit_bytes=VMEM_LIMIT_LARGE),
        name="outproj_ffn",
    )(x, o_moba, o_dil, moba_out_norm.reshape(1, half), dil_out_norm.reshape(1, half),
      w_out, ffn_norm.reshape(1, D), w_gate, w_up, w_down, final_norm.reshape(1, D))


def _rope_tables(S):
    inv_freq = ROPE_THETA ** (-np.arange(0, HEAD_DIM, 2, dtype=np.float64) / HEAD_DIM)
    ang = np.arange(S, dtype=np.float64)[:, None] * inv_freq[None, :]
    cos = np.concatenate([np.cos(ang)] * 4, axis=-1)
    sin = np.concatenate([-np.sin(ang), np.sin(ang)] * 2, axis=-1)
    return jnp.asarray(cos, F32), jnp.asarray(sin, F32)


def kernel(x, attn_norm, w_in, moba_out_norm, dil_out_norm, w_out, ffn_norm, w_gate, w_up,
           w_down, final_norm):
    B, S, D = x.shape
    assert DILATIONS == (1, 4, 16)
    assert D == D_MODEL and S % TOKEN_TILE == 0 and S % (DILATIONS[-1] * WINDOW_LEN) == 0
    assert S // MOBA_BLOCK <= LANES - ONEHOT_LANE0
    cos, sin = _rope_tables(S)
    v0 = 2 * D_MODEL
    wv_moba = lax.optimization_barrier(w_in[:, v0:v0 + N_HEADS_MOBA * HEAD_DIM])
    wv_moba_t = wv_moba.T.astype(BF16)
    qm, km, vmt, qd, kd, vd = _qkv_call(x, attn_norm, w_in.astype(BF16), wv_moba_t, cos, sin)
    o_moba = _moba_call(qm, km, vmt)
    o_dil = _dilated_call(qd, kd, vd)
    return _ffn_call(x, o_moba, o_dil, moba_out_norm, dil_out_norm, w_out.astype(BF16),
                     ffn_norm, w_gate.astype(BF16), w_up.astype(BF16), w_down.astype(BF16),
                     final_norm)
```

```python
import functools
import math

import jax
import jax.numpy as jnp
import numpy as np
from jax import lax
from jax.experimental import pallas as pl
from jax.experimental.pallas import tpu as pltpu

F32 = jnp.float32
BF16 = jnp.bfloat16

D_MODEL = 1024
HEAD_DIM = 64
N_HEADS = D_MODEL // HEAD_DIM
N_HEADS_MOBA = N_HEADS // 2
MOBA_BLOCK = 256
MOBA_TOPK = 3
DILATIONS = (1, 4, 16)
WINDOW_LEN = 128
D_FF = 2816
ROPE_THETA = 10000.0
RMS_EPS = 1e-6
NEG_INF = -1e30
ATTN_SCALE = 1.0 / math.sqrt(HEAD_DIM)
Q_SCALE = ATTN_SCALE * math.log2(math.e)

LANES = 128
F32_SUBLANES = 8
BF16_SUBLANES = 16
SUM_LANE = HEAD_DIM
VT_ROWS = HEAD_DIM + BF16_SUBLANES
ONEHOT_LANE0 = HEAD_DIM

TOKEN_TILE = 512
FFN_COL_CHUNKS = ((0, 1536), (1536, D_FF))
FFN_ROW_CHUNKS = 4
FFN_TAIL_PIECES = 2
MERGE_ROWS = 256
DILATED_GROUP = 32
HEADS_PER_STEP = 2
MOBA_STEP = 2
GATE_CHUNK = 1024
VMEM_LIMIT_SMALL = 40 * 1024 * 1024
VMEM_LIMIT_LARGE = 56 * 1024 * 1024

_NT = (((1,), (1,)), ((), ()))
_TN = (((0,), (0,)), ((), ()))


def _interleave(*generators):
    running = list(generators)
    while running:
        for g in list(running):
            if next(g, StopIteration) is StopIteration:
                running.remove(g)


def _rms(y, gain):
    return y * lax.rsqrt(jnp.mean(y * y, axis=-1, keepdims=True) + RMS_EPS) * gain


def _qkv_kernel(x_ref, g_ref, w_ref, wvt_ref, cos_ref, sin_ref,
                qm_ref, km_ref, vmt_ref, qd_ref, kd_ref, vd_ref):
    tile = x_ref.shape[1]
    moba_width = N_HEADS_MOBA * HEAD_DIM
    h = _rms(x_ref[0], g_ref[...]).astype(BF16)
    cos = cos_ref[...]
    sin = sin_ref[...]
    lane = lax.broadcasted_iota(jnp.int32, (tile, LANES), 1)
    row = lax.broadcasted_iota(jnp.int32, (tile, LANES), 0)
    low_head = lane < HEAD_DIM
    first_half = (lane & (HEAD_DIM - 1)) < HEAD_DIM // 2
    block_of_row = (pl.program_id(1) * tile + row) // MOBA_BLOCK
    onehot = jnp.where(lane - ONEHOT_LANE0 == block_of_row, 1.0, 0.0)
    ones_col = jnp.where(lane == SUM_LANE, 1.0, 0.0)
    zeros = jnp.zeros((tile, LANES), F32)

    def rope(y):
        rot = jnp.where(first_half, pltpu.roll(y, LANES - HEAD_DIM // 2, 1),
                        pltpu.roll(y, HEAD_DIM // 2, 1))
        return y * cos + rot * sin

    def split(y, pad):
        return (jnp.where(low_head, y, pad),
                jnp.where(low_head, pltpu.roll(y, HEAD_DIM, 1), pad))

    q = jnp.dot(h, w_ref[:, 0:D_MODEL], preferred_element_type=F32)
    k = jnp.dot(h, w_ref[:, D_MODEL:2 * D_MODEL], preferred_element_type=F32)
    v = jnp.dot(h, w_ref[:, 2 * D_MODEL + moba_width:3 * D_MODEL], preferred_element_type=F32)
    for pair in range(N_HEADS // 2):
        cols = slice(pair * LANES, (pair + 1) * LANES)
        moba = pair < N_HEADS_MOBA // 2
        q_ref, k_ref = (qm_ref, km_ref) if moba else (qd_ref, kd_ref)
        head0 = 2 * pair - (0 if moba else N_HEADS_MOBA)
        qs = split(rope(q[:, cols]) * Q_SCALE, zeros)
        ks = split(rope(k[:, cols]), onehot if moba else zeros)
        for a in range(2):
            q_ref[0, head0 + a] = qs[a].astype(BF16)
            k_ref[0, head0 + a] = ks[a].astype(BF16)
        if not moba:
            vs = split(v[:, head0 * HEAD_DIM:head0 * HEAD_DIM + LANES], ones_col)
            for a in range(2):
                vd_ref[0, head0 + a] = vs[a].astype(BF16)

    vt = lax.dot_general(wvt_ref[...], h, _NT, preferred_element_type=F32)
    sub = lax.broadcasted_iota(jnp.int32, (VT_ROWS - HEAD_DIM, tile), 0)
    ones_row = jnp.where(sub == SUM_LANE - HEAD_DIM, 1.0, 0.0)
    for head in range(N_HEADS_MOBA):
        slab = jnp.concatenate([vt[head * HEAD_DIM:(head + 1) * HEAD_DIM, :], ones_row],
                               axis=0).astype(BF16)
        for c in range(tile // MOBA_BLOCK):
            vmt_ref[0, head, c] = slab[:, c * MOBA_BLOCK:(c + 1) * MOBA_BLOCK]


def _qkv_call(x, attn_norm, w_in, wv_moba_t, cos, sin):
    B, S, D = x.shape
    tile = TOKEN_TILE
    H = N_HEADS_MOBA
    head_shape = jax.ShapeDtypeStruct((B, H, S, LANES), BF16)
    head_spec = pl.BlockSpec((1, H, tile, LANES), lambda b, s: (b, 0, s, 0))
    vt_shape = jax.ShapeDtypeStruct((B, H, S // MOBA_BLOCK, VT_ROWS, MOBA_BLOCK), BF16)
    vt_spec = pl.BlockSpec((1, H, tile // MOBA_BLOCK, VT_ROWS, MOBA_BLOCK),
                           lambda b, s: (b, 0, s, 0, 0))
    return pl.pallas_call(
        _qkv_kernel,
        grid=(B, S // tile),
        in_specs=[
            pl.BlockSpec((1, tile, D), lambda b, s: (b, s, 0)),
            pl.BlockSpec((1, D), lambda b, s: (0, 0)),
            pl.BlockSpec((D, 3 * D), lambda b, s: (0, 0), pipeline_mode=pl.Buffered(1)),
            pl.BlockSpec(wv_moba_t.shape, lambda b, s: (0, 0), pipeline_mode=pl.Buffered(1)),
            pl.BlockSpec((tile, LANES), lambda b, s: (s, 0)),
            pl.BlockSpec((tile, LANES), lambda b, s: (s, 0)),
        ],
        out_specs=[head_spec, head_spec, vt_spec, head_spec, head_spec, head_spec],
        out_shape=[head_shape, head_shape, vt_shape, head_shape, head_shape, head_shape],
        compiler_params=pltpu.CompilerParams(
            dimension_semantics=("arbitrary", "arbitrary"),
            vmem_limit_bytes=VMEM_LIMIT_SMALL),
        name="qkv_rope",
    )(x, attn_norm.reshape(1, D), w_in, wv_moba_t, cos, sin)


def _emit_regions(regions):
    for r, makers in enumerate(regions):
        @pl.when(pl.program_id(0) > -1 - r)
        def _(makers=makers):
            _interleave(*[make() for make in makers])


def _moba_head(hh, q_ref, k_ref, vt_ref, o_ref, qb_ref, s_ref, max_ref):
    blk = MOBA_BLOCK
    S = k_ref.shape[2]
    n_blocks = S // blk

    def prologue():
        kmean = jnp.concatenate(
            [jnp.mean(k_ref[0, hh, n * blk:(n + 1) * blk, :].astype(F32), axis=0, keepdims=True)
             for n in range(n_blocks)], axis=0)
        sub = lax.broadcasted_iota(jnp.int32, (n_blocks, LANES), 0)
        lane = lax.broadcasted_iota(jnp.int32, (n_blocks, LANES), 1)
        place = jnp.where(lane == sub + ONEHOT_LANE0, 1.0, 0.0).astype(BF16)
        k_hi = kmean.astype(BF16)
        k_mid = (kmean - k_hi.astype(F32)).astype(BF16)
        k_lo = (kmean - k_hi.astype(F32) - k_mid.astype(F32)).astype(BF16)
        k_terms = jnp.concatenate([k_hi, k_mid, k_lo], axis=0)
        gc = GATE_CHUNK
        n_idx = lax.broadcasted_iota(jnp.int32, (n_blocks, gc), 0)
        pos = lax.broadcasted_iota(jnp.int32, (n_blocks, gc), 1)
        q_lanes = lax.broadcasted_iota(jnp.int32, (gc, LANES), 1) < HEAD_DIM
        for c in range(S // gc):
            q = q_ref[0, hh, c * gc:(c + 1) * gc, :]
            g3 = lax.dot_general(k_terms, q, _NT, preferred_element_type=F32)
            gate_t = g3[:n_blocks] + g3[n_blocks:2 * n_blocks] + g3[2 * n_blocks:]
            own = (pos + c * gc) // blk
            past = n_idx < own
            gate_t = jnp.where(past, gate_t, NEG_INF)
            chosen = jnp.zeros((n_blocks, gc), F32)
            for _ in range(MOBA_TOPK):
                best = jnp.max(gate_t, axis=0, keepdims=True)
                first = jnp.min(jnp.where(gate_t == best, n_idx, n_blocks), axis=0,
                                keepdims=True)
                hit = n_idx == first
                chosen = jnp.where(hit, 1.0, chosen)
                gate_t = jnp.where(hit, -jnp.inf, gate_t)
            visible = ((chosen > 0.0) & past) | (n_idx == own)
            bias_t = jnp.where(visible, 0.0, NEG_INF).astype(BF16)
            bias = lax.dot_general(bias_t, place, _TN, preferred_element_type=F32)
            qb_ref[c * gc:(c + 1) * gc, :] = jnp.where(q_lanes, q, bias.astype(BF16))
            yield

    def score_pass(own):
        qb = qb_ref[own * blk:(own + 1) * blk, :]
        m8 = None
        for j in range(own + 1):
            s = lax.dot_general(k_ref[0, hh, j * blk:(j + 1) * blk, :], qb, _NT,
                                preferred_element_type=F32)
            if j == own:
                key_idx = lax.broadcasted_iota(jnp.int32, (blk, blk), 0)
                qry_idx = lax.broadcasted_iota(jnp.int32, (blk, blk), 1)
                s = jnp.where(key_idx <= qry_idx, s, NEG_INF)
            s_ref[own % (2 * MOBA_STEP), j * blk:(j + 1) * blk, :] = s
            s8 = jnp.max(s.reshape(blk // F32_SUBLANES, F32_SUBLANES, blk), axis=0)
            m8 = s8 if m8 is None else jnp.maximum(m8, s8)
            yield
        max_ref[own:own + 1, :] = jnp.max(m8, axis=0, keepdims=True)

    def value_pass(own):
        m = max_ref[own:own + 1, :]
        acc_t = None
        for j in range(own + 1):
            p = jnp.exp2(s_ref[own % (2 * MOBA_STEP), j * blk:(j + 1) * blk, :] - m).astype(BF16)
            part = jnp.dot(vt_ref[0, hh, j], p, preferred_element_type=F32)
            acc_t = part if acc_t is None else acc_t + part
            yield
        out_t = acc_t / acc_t[SUM_LANE:SUM_LANE + 1, :]
        out_t = jnp.concatenate([out_t, jnp.zeros((LANES - VT_ROWS, blk), F32)], axis=0)
        o_ref[0, hh, own * blk:(own + 1) * blk, :] = out_t.T.astype(BF16)

    n_steps = n_blocks // MOBA_STEP
    steps = []
    for t in range(n_steps + 1):
        parts = []
        for i in range(MOBA_STEP):
            if t > 0:
                parts.append(functools.partial(value_pass, (t - 1) * MOBA_STEP + i))
            if t < n_steps:
                parts.append(functools.partial(score_pass, t * MOBA_STEP + i))
        steps.append(parts)
    return prologue, steps


def _moba_kernel(*refs):
    heads = [_moba_head(hh, *refs) for hh in range(HEADS_PER_STEP)]
    regions = []
    for hh, (prologue, steps) in enumerate(heads):
        steps = [list(parts) for parts in steps]
        if hh == 0:
            steps[0].insert(0, prologue)
        if hh + 1 < len(heads):
            steps[-1].append(heads[hh + 1][0])
        regions.extend(steps)
    _emit_regions(regions)


def _moba_call(q, k, vt):
    B, H, S, _ = q.shape
    blk = MOBA_BLOCK
    assert S % GATE_CHUNK == 0 and H % HEADS_PER_STEP == 0
    seq_spec = pl.BlockSpec((1, HEADS_PER_STEP, S, LANES), lambda b, h: (b, h, 0, 0))
    vt_spec = pl.BlockSpec((1, HEADS_PER_STEP, S // blk, VT_ROWS, blk),
                           lambda b, h: (b, h, 0, 0, 0))
    return pl.pallas_call(
        _moba_kernel,
        grid=(B, H // HEADS_PER_STEP),
        in_specs=[seq_spec, seq_spec, vt_spec],
        out_specs=seq_spec,
        out_shape=jax.ShapeDtypeStruct((B, H, S, LANES), BF16),
        scratch_shapes=[pltpu.VMEM((S, LANES), BF16),
                        pltpu.VMEM((2 * MOBA_STEP, S, blk), F32),
                        pltpu.VMEM((S // blk, blk), F32)],
        compiler_params=pltpu.CompilerParams(
            dimension_semantics=("arbitrary", "arbitrary"),
            vmem_limit_bytes=VMEM_LIMIT_SMALL),
        name="moba_attention",
    )(q, k, vt)


def _dilated_head(hh, q_ref, k_ref, v_ref, o_ref, stage_ref, stage4_ref,
                  q4_ref, k4_ref, v4_ref, q16_ref, k16_ref, v16_ref,
                  acc1_ref, acc4_ref, acc16_ref, max1_ref, max4_ref, max16_ref,
                  s_ref, m_ref):
    S = q_ref.shape[2]
    L = WINDOW_LEN
    q1_ref, k1_ref, v1_ref = q_ref.at[0, hh], k_ref.at[0, hh], v_ref.at[0, hh]

    n4, n16 = S // 4, S // 16

    def deinterleave(src, dst4, dst16):
        stage_ref[...] = src[...].astype(F32)
        for r in range(4):
            x = stage_ref[pl.ds(r, n4, stride=4), :]
            dst4[r * n4:(r + 1) * n4, :] = x.astype(BF16)
            stage4_ref[r * n4:(r + 1) * n4, :] = x
        for r in range(16):
            x = stage4_ref[pl.ds((r % 4) * n4 + r // 4, n16, stride=4), :]
            dst16[r * n16:(r + 1) * n16, :] = x.astype(BF16)
        yield

    acc16t_ref, max16t_ref = stage_ref, stage4_ref

    def order16():
        for r in range(4):
            acc16_ref[pl.ds(r, n4, stride=4), :] = acc16t_ref[r * n4:(r + 1) * n4, :]
            max16_ref[pl.ds(r, n4, stride=4), :] = max16t_ref[r * n4:(r + 1) * n4, :]
        yield

    tiles = []
    for d, qd_ref, kd_ref, vd_ref, acc_ref, max_ref in (
            (1, q1_ref, k1_ref, v1_ref, acc1_ref, max1_ref),
            (4, q4_ref, k4_ref, v4_ref, acc4_ref, max4_ref),
            (16, q16_ref, k16_ref, v16_ref, acc16t_ref, max16t_ref)):
        n = S // d
        for residue in range(d):
            for t in range(n // L):
                base = residue * n + t * L
                keys = slice(base - L if t > 0 else base, base + L)
                if d == 1:
                    rows = slice(base, base + L)
                elif d == 4:
                    rows = pl.ds(t * L * 4 + residue, L, stride=4)
                else:
                    rows = pl.ds((residue % 4) * n4 + t * L * 4 + residue // 4, L, stride=4)
                tiles.append((qd_ref, kd_ref, vd_ref, acc_ref, max_ref, base, keys, rows))
    group = DILATED_GROUP
    n_groups = len(tiles) // group

    def score_pass(g):
        for i, (qd_ref, kd_ref, _, _, max_ref, base, keys, rows) in enumerate(
                tiles[g * group:(g + 1) * group]):
            width = keys.stop - keys.start
            s = lax.dot_general(qd_ref[base:base + L, :], kd_ref[keys, :], _NT,
                                preferred_element_type=F32)
            r_idx = lax.broadcasted_iota(jnp.int32, (L, width), 0)
            c_idx = lax.broadcasted_iota(jnp.int32, (L, width), 1)
            if width == 2 * L:
                seen = ((c_idx < L) & (c_idx >= r_idx)) | ((c_idx >= L) & (c_idx - L <= r_idx))
            else:
                seen = c_idx <= r_idx
            s = jnp.where(seen, s, NEG_INF)
            m = jnp.max(s, axis=1, keepdims=True)
            m = jnp.broadcast_to(m, (L, LANES))
            s_ref[g % 2, i * L:(i + 1) * L, 0:width] = s
            m_ref[g % 2, i * L:(i + 1) * L, :] = m
            max_ref[rows, :] = m
            yield

    def value_pass(g):
        for i, (_, _, vd_ref, acc_ref, _, _, keys, rows) in enumerate(
                tiles[g * group:(g + 1) * group]):
            width = keys.stop - keys.start
            m = m_ref[g % 2, i * L:(i + 1) * L, :]
            p = jnp.concatenate(
                [jnp.exp2(s_ref[g % 2, i * L:(i + 1) * L, c * L:(c + 1) * L] - m)
                 for c in range(width // L)], axis=1).astype(BF16)
            acc_ref[rows, :] = jnp.dot(p, vd_ref[keys, :], preferred_element_type=F32)
            yield

    def merge():
        for c in range(S // MERGE_ROWS):
            rows = slice(c * MERGE_ROWS, (c + 1) * MERGE_ROWS)
            maxes = [ref[rows, :] for ref in (max1_ref, max4_ref, max16_ref)]
            m = jnp.maximum(jnp.maximum(maxes[0], maxes[1]), maxes[2])
            total = sum(jnp.exp2(mi - m) * ref[rows, :]
                        for mi, ref in zip(maxes, (acc1_ref, acc4_ref, acc16_ref)))
            o_ref[0, hh, rows, :] = (total / total[:, SUM_LANE:SUM_LANE + 1]).astype(BF16)
            yield

    first_dilated = (S // L) // group
    assert first_dilated >= 1
    steps = []
    for t in range(n_groups + 2):
        parts = []
        if t == 1:
            parts.append(functools.partial(deinterleave, v1_ref, v4_ref, v16_ref))
        if 0 < t <= n_groups:
            parts.append(functools.partial(value_pass, t - 1))
        if t < n_groups:
            parts.append(functools.partial(score_pass, t))
        if t == n_groups + 1:
            parts.extend([order16, merge])
        steps.append(parts)
    qk_copies = [functools.partial(deinterleave, q1_ref, q4_ref, q16_ref),
                 functools.partial(deinterleave, k1_ref, k4_ref, k16_ref)]
    return qk_copies, steps


def _dilated_kernel(*refs):
    heads = [_dilated_head(hh, *refs) for hh in range(HEADS_PER_STEP)]
    regions = []
    for hh, (qk_copies, steps) in enumerate(heads):
        steps = [list(parts) for parts in steps]
        if hh == 0:
            steps[0] = qk_copies + steps[0]
        if hh + 1 < len(heads):
            steps[-1].extend(heads[hh + 1][0])
        regions.extend(steps)
    _emit_regions(regions)


def _dilated_call(q, k, v):
    B, H, S, _ = q.shape
    assert H % HEADS_PER_STEP == 0
    seq_spec = pl.BlockSpec((1, HEADS_PER_STEP, S, LANES), lambda b, h: (b, h, 0, 0))
    return pl.pallas_call(
        _dilated_kernel,
        grid=(B, H // HEADS_PER_STEP),
        in_specs=[seq_spec] * 3,
        out_specs=seq_spec,
        out_shape=jax.ShapeDtypeStruct((B, H, S, LANES), BF16),
        scratch_shapes=([pltpu.VMEM((S, LANES), F32)] * 2
                        + [pltpu.VMEM((S, LANES), BF16)] * 6
                        + [pltpu.VMEM((S, LANES), F32)] * 6
                        + [pltpu.VMEM((2, DILATED_GROUP * WINDOW_LEN, 2 * WINDOW_LEN), F32),
                           pltpu.VMEM((2, DILATED_GROUP * WINDOW_LEN, LANES), F32)]),
        compiler_params=pltpu.CompilerParams(
            dimension_semantics=("arbitrary", "arbitrary"),
            vmem_limit_bytes=VMEM_LIMIT_LARGE),
        name="dilated_attention",
    )(q, k, v)


def _ffn_kernel(x_ref, om_ref, od_ref, gm_ref, gd_ref, wo_ref, gf_ref,
                wg_ref, wu_ref, wd_ref, gl_ref, out_ref):
    tile = x_ref.shape[1]
    sub = tile // FFN_ROW_CHUNKS
    lane = lax.broadcasted_iota(jnp.int32, (sub, LANES), 1)
    low_head = lane < HEAD_DIM

    def heads(ref, rows):
        pairs = []
        for p in range(ref.shape[1] // 2):
            a = ref[0, 2 * p, rows, :].astype(F32)
            b = ref[0, 2 * p + 1, rows, :].astype(F32)
            pairs.append(jnp.where(low_head, a, pltpu.roll(b, HEAD_DIM, 1)))
        return jnp.concatenate(pairs, axis=1)

    x1_parts, h_parts = [], []
    for r in range(FFN_ROW_CHUNKS):
        rows = slice(r * sub, (r + 1) * sub)
        mixed = jnp.concatenate([_rms(heads(om_ref, rows), gm_ref[...]),
                                 _rms(heads(od_ref, rows), gd_ref[...])], axis=1).astype(BF16)
        x1 = x_ref[0, rows, :] + jnp.dot(mixed, wo_ref[...], preferred_element_type=F32)
        x1_parts.append(x1)
        h_parts.append(_rms(x1, gf_ref[...]).astype(BF16))
    x2 = jnp.concatenate(x1_parts, axis=0)
    h = jnp.concatenate(h_parts, axis=0)
    for lo, hi in FFN_COL_CHUNKS[:-1]:
        g = jnp.dot(h, wg_ref[:, lo:hi], preferred_element_type=F32)
        u = jnp.dot(h, wu_ref[:, lo:hi], preferred_element_type=F32)
        act = (g * jax.nn.sigmoid(g) * u).astype(BF16)
        x2 = x2 + jnp.dot(act, wd_ref[lo:hi, :], preferred_element_type=F32)
    lo, hi = FFN_COL_CHUNKS[-1]
    g = jnp.dot(h, wg_ref[:, lo:hi], preferred_element_type=F32)
    u = jnp.dot(h, wu_ref[:, lo:hi], preferred_element_type=F32)
    act = (g * jax.nn.sigmoid(g) * u).astype(BF16)
    piece = tile // FFN_TAIL_PIECES
    for r in range(FFN_TAIL_PIECES):
        rows = slice(r * piece, (r + 1) * piece)
        y = x2[rows, :] + jnp.dot(act[rows, :], wd_ref[lo:hi, :], preferred_element_type=F32)
        out_ref[0, rows, :] = _rms(y, gl_ref[...])


def _ffn_call(x, o_moba, o_dil, moba_out_norm, dil_out_norm, w_out, ffn_norm,
              w_gate, w_up, w_down, final_norm):
    B, S, D = x.shape
    tile = TOKEN_TILE
    half = N_HEADS_MOBA * HEAD_DIM

    def const(shape):
        return pl.BlockSpec(shape, lambda b, s: (0,) * len(shape), pipeline_mode=pl.Buffered(1))

    row_spec = pl.BlockSpec((1, tile, D), lambda b, s: (b, s, 0))
    head_spec = pl.BlockSpec((1, N_HEADS_MOBA, tile, LANES), lambda b, s: (b, 0, s, 0))
    return pl.pallas_call(
        _ffn_kernel,
        grid=(B, S // tile),
        in_specs=[row_spec, head_spec, head_spec,
                  const((1, half)), const((1, half)), const((D, D)), const((1, D)),
                  const((D, D_FF)), const((D, D_FF)), const((D_FF, D)), const((1, D))],
        out_specs=row_spec,
        out_shape=jax.ShapeDtypeStruct((B, S, D), F32),
        compiler_params=pltpu.CompilerParams(
            dimension_semantics=("arbitrary", "arbitrary"),
            vmem_limit_bytes=VMEM_LIMIT_LARGE),
        name="outproj_ffn",
    )(x, o_moba, o_dil, moba_out_norm.reshape(1, half), dil_out_norm.reshape(1, half),
      w_out, ffn_norm.reshape(1, D), w_gate, w_up, w_down, final_norm.reshape(1, D))


def _rope_tables(S):
    inv_freq = ROPE_THETA ** (-np.arange(0, HEAD_DIM, 2, dtype=np.float64) / HEAD_DIM)
    ang = np.arange(S, dtype=np.float64)[:, None] * inv_freq[None, :]
    cos = np.concatenate([np.cos(ang)] * 4, axis=-1)
    sin = np.concatenate([-np.sin(ang), np.sin(ang)] * 2, axis=-1)
    return jnp.asarray(cos, F32), jnp.asarray(sin, F32)


def kernel(x, attn_norm, w_in, moba_out_norm, dil_out_norm, w_out, ffn_norm, w_gate, w_up,
           w_down, final_norm):
    B, S, D = x.shape
    assert DILATIONS == (1, 4, 16)
    assert D == D_MODEL and S % TOKEN_TILE == 0 and S % (DILATIONS[-1] * WINDOW_LEN) == 0
    assert S // MOBA_BLOCK <= LANES - ONEHOT_LANE0
    cos, sin = _rope_tables(S)
    v0 = 2 * D_MODEL
    wv_moba = lax.optimization_barrier(w_in[:, v0:v0 + N_HEADS_MOBA * HEAD_DIM])
    wv_moba_t = wv_moba.T.astype(BF16)
    qm, km, vmt, qd, kd, vd = _qkv_call(x, attn_norm, w_in.astype(BF16), wv_moba_t, cos, sin)
    o_moba = _moba_call(qm, km, vmt)
    o_dil = _dilated_call(qd, kd, vd)
    return _ffn_call(x, o_moba, o_dil, moba_out_norm, dil_out_norm, w_out.astype(BF16),
                     ffn_norm, w_gate.astype(BF16), w_up.astype(BF16), w_down.astype(BF16),
                     final_norm)
```

```python
import math

import jax
import jax.numpy as jnp
import numpy as np
from jax import lax
from jax.experimental import pallas as pl
from jax.experimental.pallas import tpu as pltpu

F32 = jnp.float32
BF16 = jnp.bfloat16

D_MODEL = 1024
HEAD_DIM = 64
N_HEADS = D_MODEL // HEAD_DIM
N_HEADS_MOBA = N_HEADS // 2
MOBA_BLOCK = 256
MOBA_TOPK = 3
DILATIONS = (1, 4, 16)
WINDOW_LEN = 128
D_FF = 2816
ROPE_THETA = 10000.0
RMS_EPS = 1e-6
NEG_INF = -1e30
ATTN_SCALE = 1.0 / math.sqrt(HEAD_DIM)
Q_SCALE = ATTN_SCALE * math.log2(math.e)

LANES = 128
F32_SUBLANES = 8
BF16_SUBLANES = 16
SUM_LANE = HEAD_DIM
VT_ROWS = HEAD_DIM + BF16_SUBLANES
ONEHOT_LANE0 = HEAD_DIM

QKV_TILE = 1024
TOKEN_TILE = 512
FFN_COL_CHUNKS = ((0, 1536), (1536, D_FF))
FFN_ROW_CHUNKS = 4
FFN_TAIL_PIECES = 2
MERGE_ROWS = 256
DILATED_GROUP = 32
MOBA_GROUPS = ((0, 1, 2, 3, 4, 5, 6, 7), (8, 9, 10, 11), (12, 13), (14, 15))
GATE_CHUNK = 1024
VMEM_LIMIT_LARGE = 56 * 1024 * 1024

_NT = (((1,), (1,)), ((), ()))
_TN = (((0,), (0,)), ((), ()))


def _interleave(*generators):
    running = list(generators)
    while running:
        for g in list(running):
            if next(g, StopIteration) is StopIteration:
                running.remove(g)


def _rms(y, gain):
    return y * lax.rsqrt(jnp.mean(y * y, axis=-1, keepdims=True) + RMS_EPS) * gain


def _qkv_kernel(x_ref, g_ref, w_ref, wvt_ref, cos_ref, sin_ref,
                qm_ref, km_ref, vmt_ref, qd_ref, kd_ref, vd_ref):
    tile = x_ref.shape[1]
    moba_width = N_HEADS_MOBA * HEAD_DIM
    h = _rms(x_ref[0], g_ref[...]).astype(BF16)
    cos = cos_ref[...]
    sin = sin_ref[...]
    lane = lax.broadcasted_iota(jnp.int32, (tile, LANES), 1)
    row = lax.broadcasted_iota(jnp.int32, (tile, LANES), 0)
    low_head = lane < HEAD_DIM
    first_half = (lane & (HEAD_DIM - 1)) < HEAD_DIM // 2
    block_of_row = (pl.program_id(1) * tile + row) // MOBA_BLOCK
    onehot = jnp.where(lane - ONEHOT_LANE0 == block_of_row, 1.0, 0.0)
    ones_col = jnp.where(lane == SUM_LANE, 1.0, 0.0)
    zeros = jnp.zeros((tile, LANES), F32)

    def rope(y):
        rot = jnp.where(first_half, pltpu.roll(y, LANES - HEAD_DIM // 2, 1),
                        pltpu.roll(y, HEAD_DIM // 2, 1))
        return y * cos + rot * sin

    def split(y, pad):
        return (jnp.where(low_head, y, pad),
                jnp.where(low_head, pltpu.roll(y, HEAD_DIM, 1), pad))

    q = jnp.dot(h, w_ref[:, 0:D_MODEL], preferred_element_type=F32)
    k = jnp.dot(h, w_ref[:, D_MODEL:2 * D_MODEL], preferred_element_type=F32)
    v = jnp.dot(h, w_ref[:, 2 * D_MODEL + moba_width:3 * D_MODEL], preferred_element_type=F32)
    for pair in range(N_HEADS // 2):
        cols = slice(pair * LANES, (pair + 1) * LANES)
        moba = pair < N_HEADS_MOBA // 2
        q_ref, k_ref = (qm_ref, km_ref) if moba else (qd_ref, kd_ref)
        head0 = 2 * pair - (0 if moba else N_HEADS_MOBA)
        qs = split(rope(q[:, cols]) * Q_SCALE, zeros)
        ks = split(rope(k[:, cols]), onehot if moba else zeros)
        for a in range(2):
            q_ref[0, head0 + a] = qs[a].astype(BF16)
            k_ref[0, head0 + a] = ks[a].astype(BF16)
        if not moba:
            vs = split(v[:, head0 * HEAD_DIM:head0 * HEAD_DIM + LANES], ones_col)
            for a in range(2):
                vd_ref[0, head0 + a] = vs[a].astype(BF16)

    vt = lax.dot_general(wvt_ref[...], h, _NT, preferred_element_type=F32)
    sub = lax.broadcasted_iota(jnp.int32, (VT_ROWS - HEAD_DIM, tile), 0)
    ones_row = jnp.where(sub == SUM_LANE - HEAD_DIM, 1.0, 0.0)
    for head in range(N_HEADS_MOBA):
        slab = jnp.concatenate([vt[head * HEAD_DIM:(head + 1) * HEAD_DIM, :], ones_row],
                               axis=0).astype(BF16)
        for c in range(tile // MOBA_BLOCK):
            vmt_ref[0, head, c] = slab[:, c * MOBA_BLOCK:(c + 1) * MOBA_BLOCK]


def _qkv_call(x, attn_norm, w_in, wv_moba_t, cos, sin):
    B, S, D = x.shape
    tile = QKV_TILE
    H = N_HEADS_MOBA
    head_shape = jax.ShapeDtypeStruct((B, H, S, LANES), BF16)
    head_spec = pl.BlockSpec((1, H, tile, LANES), lambda b, s: (b, 0, s, 0))
    vt_shape = jax.ShapeDtypeStruct((B, H, S // MOBA_BLOCK, VT_ROWS, MOBA_BLOCK), BF16)
    vt_spec = pl.BlockSpec((1, H, tile // MOBA_BLOCK, VT_ROWS, MOBA_BLOCK),
                           lambda b, s: (b, 0, s, 0, 0))
    return pl.pallas_call(
        _qkv_kernel,
        grid=(B, S // tile),
        in_specs=[
            pl.BlockSpec((1, tile, D), lambda b, s: (b, s, 0)),
            pl.BlockSpec((1, D), lambda b, s: (0, 0)),
            pl.BlockSpec((D, 3 * D), lambda b, s: (0, 0), pipeline_mode=pl.Buffered(1)),
            pl.BlockSpec(wv_moba_t.shape, lambda b, s: (0, 0), pipeline_mode=pl.Buffered(1)),
            pl.BlockSpec((tile, LANES), lambda b, s: (s, 0)),
            pl.BlockSpec((tile, LANES), lambda b, s: (s, 0)),
        ],
        out_specs=[head_spec, head_spec, vt_spec, head_spec, head_spec, head_spec],
        out_shape=[head_shape, head_shape, vt_shape, head_shape, head_shape, head_shape],
        compiler_params=pltpu.CompilerParams(
            dimension_semantics=("arbitrary", "arbitrary"),
            vmem_limit_bytes=VMEM_LIMIT_LARGE),
        name="qkv_rope",
    )(x, attn_norm.reshape(1, D), w_in, wv_moba_t, cos, sin)


def _moba_kernel(q_ref, k_ref, vt_ref, o_ref, qb_ref, s_ref, max_ref):
    blk = MOBA_BLOCK
    S = k_ref.shape[2]
    n_blocks = S // blk

    kmean = jnp.concatenate(
        [jnp.mean(k_ref[0, 0, n * blk:(n + 1) * blk, :].astype(F32), axis=0, keepdims=True)
         for n in range(n_blocks)], axis=0)
    sub = lax.broadcasted_iota(jnp.int32, (n_blocks, LANES), 0)
    lane = lax.broadcasted_iota(jnp.int32, (n_blocks, LANES), 1)
    place = jnp.where(lane == sub + ONEHOT_LANE0, 1.0, 0.0).astype(BF16)
    k_hi = kmean.astype(BF16)
    k_mid = (kmean - k_hi.astype(F32)).astype(BF16)
    k_lo = (kmean - k_hi.astype(F32) - k_mid.astype(F32)).astype(BF16)
    k_terms = jnp.concatenate([k_hi, k_mid, k_lo], axis=0)
    gc = GATE_CHUNK
    n_idx = lax.broadcasted_iota(jnp.int32, (n_blocks, gc), 0)
    pos = lax.broadcasted_iota(jnp.int32, (n_blocks, gc), 1)
    q_lanes = lax.broadcasted_iota(jnp.int32, (gc, LANES), 1) < HEAD_DIM
    for c in range(S // gc):
        q = q_ref[0, 0, c * gc:(c + 1) * gc, :]
        g3 = lax.dot_general(k_terms, q, _NT, preferred_element_type=F32)
        gate_t = g3[:n_blocks] + g3[n_blocks:2 * n_blocks] + g3[2 * n_blocks:]
        own = (pos + c * gc) // blk
        past = n_idx < own
        gate_t = jnp.where(past, gate_t, NEG_INF)
        chosen = jnp.zeros((n_blocks, gc), F32)
        for _ in range(MOBA_TOPK):
            best = jnp.max(gate_t, axis=0, keepdims=True)
            first = jnp.min(jnp.where(gate_t == best, n_idx, n_blocks), axis=0, keepdims=True)
            hit = n_idx == first
            chosen = jnp.where(hit, 1.0, chosen)
            gate_t = jnp.where(hit, -jnp.inf, gate_t)
        visible = ((chosen > 0.0) & past) | (n_idx == own)
        bias_t = jnp.where(visible, 0.0, NEG_INF).astype(BF16)
        bias = lax.dot_general(bias_t, place, _TN, preferred_element_type=F32)
        qb_ref[c * gc:(c + 1) * gc, :] = jnp.where(q_lanes, q, bias.astype(BF16))

    def base_of(own):
        return own * (own + 1) // 2

    def score_pass(own):
        qb = qb_ref[own * blk:(own + 1) * blk, :]
        m8 = None
        for j in range(own + 1):
            s = lax.dot_general(k_ref[0, 0, j * blk:(j + 1) * blk, :], qb, _NT,
                                preferred_element_type=F32)
            if j == own:
                key_idx = lax.broadcasted_iota(jnp.int32, (blk, blk), 0)
                qry_idx = lax.broadcasted_iota(jnp.int32, (blk, blk), 1)
                s = jnp.where(key_idx <= qry_idx, s, NEG_INF)
            s_ref[(base_of(own) + j) * blk:(base_of(own) + j + 1) * blk, :] = s
            s8 = jnp.max(s.reshape(blk // F32_SUBLANES, F32_SUBLANES, blk), axis=0)
            m8 = s8 if m8 is None else jnp.maximum(m8, s8)
            yield
        max_ref[own:own + 1, :] = jnp.max(m8, axis=0, keepdims=True)

    def value_pass(own):
        m = max_ref[own:own + 1, :]
        acc_t = None
        for j in range(own + 1):
            s = s_ref[(base_of(own) + j) * blk:(base_of(own) + j + 1) * blk, :]
            p = jnp.exp2(s - m).astype(BF16)
            part = jnp.dot(vt_ref[0, 0, j], p, preferred_element_type=F32)
            acc_t = part if acc_t is None else acc_t + part
            yield
        out_t = acc_t / acc_t[SUM_LANE:SUM_LANE + 1, :]
        out_t = jnp.concatenate([out_t, jnp.zeros((LANES - VT_ROWS, blk), F32)], axis=0)
        o_ref[0, 0, own * blk:(own + 1) * blk, :] = out_t.T.astype(BF16)

    groups = MOBA_GROUPS
    assert [own for g in groups for own in g] == list(range(n_blocks))
    for t in range(len(groups) + 1):
        @pl.when(pl.program_id(0) > -1 - t)
        def _(t=t):
            parts = []
            if t > 0:
                parts += [value_pass(own) for own in groups[t - 1]]
            if t < len(groups):
                parts += [score_pass(own) for own in groups[t]]
            _interleave(*parts)


def _moba_call(q, k, vt):
    B, H, S, _ = q.shape
    blk = MOBA_BLOCK
    assert S % GATE_CHUNK == 0
    seq_spec = pl.BlockSpec((1, 1, S, LANES), lambda b, h: (b, h, 0, 0))
    vt_spec = pl.BlockSpec((1, 1, S // blk, VT_ROWS, blk), lambda b, h: (b, h, 0, 0, 0))
    return pl.pallas_call(
        _moba_kernel,
        grid=(B, H),
        in_specs=[seq_spec, seq_spec, vt_spec],
        out_specs=seq_spec,
        out_shape=jax.ShapeDtypeStruct((B, H, S, LANES), BF16),
        scratch_shapes=[pltpu.VMEM((S, LANES), BF16),
                        pltpu.VMEM(((S // blk) * (S // blk + 1) // 2 * blk, blk), F32),
                        pltpu.VMEM((S // blk, blk), F32)],
        compiler_params=pltpu.CompilerParams(
            dimension_semantics=("arbitrary", "arbitrary"),
            vmem_limit_bytes=VMEM_LIMIT_LARGE),
        name="moba_attention",
    )(q, k, vt)


def _dilated_kernel(q_ref, k_ref, v_ref, o_ref, stage_ref, stage4_ref,
                    q4_ref, k4_ref, v4_ref, q16_ref, k16_ref, v16_ref,
                    acc1_ref, acc4_ref, acc16_ref, max1_ref, max4_ref, max16_ref,
                    s_ref, m_ref):
    S = q_ref.shape[2]
    L = WINDOW_LEN
    q1_ref, k1_ref, v1_ref = q_ref.at[0, 0], k_ref.at[0, 0], v_ref.at[0, 0]

    n4, n16 = S // 4, S // 16

    def deinterleave(src, dst4, dst16):
        stage_ref[...] = src[...].astype(F32)
        for r in range(4):
            x = stage_ref[pl.ds(r, n4, stride=4), :]
            dst4[r * n4:(r + 1) * n4, :] = x.astype(BF16)
            stage4_ref[r * n4:(r + 1) * n4, :] = x
        for r in range(16):
            x = stage4_ref[pl.ds((r % 4) * n4 + r // 4, n16, stride=4), :]
            dst16[r * n16:(r + 1) * n16, :] = x.astype(BF16)
        yield

    acc16t_ref, max16t_ref = stage_ref, stage4_ref

    def order16():
        for r in range(4):
            acc16_ref[pl.ds(r, n4, stride=4), :] = acc16t_ref[r * n4:(r + 1) * n4, :]
            max16_ref[pl.ds(r, n4, stride=4), :] = max16t_ref[r * n4:(r + 1) * n4, :]
        yield

    tiles = []
    for d, qd_ref, kd_ref, vd_ref, acc_ref, max_ref in (
            (1, q1_ref, k1_ref, v1_ref, acc1_ref, max1_ref),
            (4, q4_ref, k4_ref, v4_ref, acc4_ref, max4_ref),
            (16, q16_ref, k16_ref, v16_ref, acc16t_ref, max16t_ref)):
        n = S // d
        for residue in range(d):
            for t in range(n // L):
                base = residue * n + t * L
                keys = slice(base - L if t > 0 else base, base + L)
                if d == 1:
                    rows = slice(base, base + L)
                elif d == 4:
                    rows = pl.ds(t * L * 4 + residue, L, stride=4)
                else:
                    rows = pl.ds((residue % 4) * n4 + t * L * 4 + residue // 4, L, stride=4)
                tiles.append((qd_ref, kd_ref, vd_ref, acc_ref, max_ref, base, keys, rows))
    group = DILATED_GROUP
    n_groups = len(tiles) // group

    def score_pass(g):
        for i, (qd_ref, kd_ref, _, _, max_ref, base, keys, rows) in enumerate(
                tiles[g * group:(g + 1) * group]):
            width = keys.stop - keys.start
            s = lax.dot_general(qd_ref[base:base + L, :], kd_ref[keys, :], _NT,
                                preferred_element_type=F32)
            r_idx = lax.broadcasted_iota(jnp.int32, (L, width), 0)
            c_idx = lax.broadcasted_iota(jnp.int32, (L, width), 1)
            if width == 2 * L:
                seen = ((c_idx < L) & (c_idx >= r_idx)) | ((c_idx >= L) & (c_idx - L <= r_idx))
            else:
                seen = c_idx <= r_idx
            s = jnp.where(seen, s, NEG_INF)
            m = jnp.max(s, axis=1, keepdims=True)
            m = jnp.broadcast_to(m, (L, LANES))
            s_ref[g % 2, i * L:(i + 1) * L, 0:width] = s
            m_ref[g % 2, i * L:(i + 1) * L, :] = m
            max_ref[rows, :] = m
            yield

    def value_pass(g):
        for i, (_, _, vd_ref, acc_ref, _, _, keys, rows) in enumerate(
                tiles[g * group:(g + 1) * group]):
            width = keys.stop - keys.start
            m = m_ref[g % 2, i * L:(i + 1) * L, :]
            p = jnp.concatenate(
                [jnp.exp2(s_ref[g % 2, i * L:(i + 1) * L, c * L:(c + 1) * L] - m)
                 for c in range(width // L)], axis=1).astype(BF16)
            acc_ref[rows, :] = jnp.dot(p, vd_ref[keys, :], preferred_element_type=F32)
            yield

    def merge():
        for c in range(S // MERGE_ROWS):
            rows = slice(c * MERGE_ROWS, (c + 1) * MERGE_ROWS)
            maxes = [ref[rows, :] for ref in (max1_ref, max4_ref, max16_ref)]
            m = jnp.maximum(jnp.maximum(maxes[0], maxes[1]), maxes[2])
            total = sum(jnp.exp2(mi - m) * ref[rows, :]
                        for mi, ref in zip(maxes, (acc1_ref, acc4_ref, acc16_ref)))
            o_ref[0, 0, rows, :] = (total / total[:, SUM_LANE:SUM_LANE + 1]).astype(BF16)
            yield

    copies = [(q1_ref, q4_ref, q16_ref), (k1_ref, k4_ref, k16_ref), (v1_ref, v4_ref, v16_ref)]
    first_dilated = (S // L) // group
    assert first_dilated >= 1
    copy_step = [0, min(1, first_dilated - 1), min(2, first_dilated)]
    for t in range(n_groups + 2):
        @pl.when(pl.program_id(0) > -1 - t)
        def _(t=t):
            parts = []
            if 0 < t <= n_groups:
                parts.append(value_pass(t - 1))
            if t < n_groups:
                parts.append(score_pass(t))
            parts.extend(deinterleave(*c) for c, at in zip(copies, copy_step) if at == t)
            if t == n_groups + 1:
                _interleave(order16())
                parts.append(merge())
            _interleave(*parts)


def _dilated_call(q, k, v):
    B, H, S, _ = q.shape
    seq_spec = pl.BlockSpec((1, 1, S, LANES), lambda b, h: (b, h, 0, 0))
    return pl.pallas_call(
        _dilated_kernel,
        grid=(B, H),
        in_specs=[seq_spec] * 3,
        out_specs=seq_spec,
        out_shape=jax.ShapeDtypeStruct((B, H, S, LANES), BF16),
        scratch_shapes=([pltpu.VMEM((S, LANES), F32)] * 2
                        + [pltpu.VMEM((S, LANES), BF16)] * 6
                        + [pltpu.VMEM((S, LANES), F32)] * 6
                        + [pltpu.VMEM((2, DILATED_GROUP * WINDOW_LEN, 2 * WINDOW_LEN), F32),
                           pltpu.VMEM((2, DILATED_GROUP * WINDOW_LEN, LANES), F32)]),
        compiler_params=pltpu.CompilerParams(
            dimension_semantics=("arbitrary", "arbitrary"),
            vmem_limit_bytes=VMEM_LIMIT_LARGE),
        name="dilated_attention",
    )(q, k, v)


def _ffn_kernel(x_ref, om_ref, od_ref, gm_ref, gd_ref, wo_ref, gf_ref,
                wg_ref, wu_ref, wd_ref, gl_ref, out_ref):
    tile = x_ref.shape[1]
    sub = tile // FFN_ROW_CHUNKS
    lane = lax.broadcasted_iota(jnp.int32, (sub, LANES), 1)
    low_head = lane < HEAD_DIM

    def heads(ref, rows):
        pairs = []
        for p in range(ref.shape[1] // 2):
            a = ref[0, 2 * p, rows, :].astype(F32)
            b = ref[0, 2 * p + 1, rows, :].astype(F32)
            pairs.append(jnp.where(low_head, a, pltpu.roll(b, HEAD_DIM, 1)))
        return jnp.concatenate(pairs, axis=1)

    x1_parts, h_parts = [], []
    for r in range(FFN_ROW_CHUNKS):
        rows = slice(r * sub, (r + 1) * sub)
        mixed = jnp.concatenate([_rms(heads(om_ref, rows), gm_ref[...]),
                                 _rms(heads(od_ref, rows), gd_ref[...])], axis=1).astype(BF16)
        x1 = x_ref[0, rows, :] + jnp.dot(mixed, wo_ref[...], preferred_element_type=F32)
        x1_parts.append(x1)
        h_parts.append(_rms(x1, gf_ref[...]).astype(BF16))
    x2 = jnp.concatenate(x1_parts, axis=0)
    h = jnp.concatenate(h_parts, axis=0)
    for lo, hi in FFN_COL_CHUNKS[:-1]:
        g = jnp.dot(h, wg_ref[:, lo:hi], preferred_element_type=F32)
        u = jnp.dot(h, wu_ref[:, lo:hi], preferred_element_type=F32)
        act = (g * jax.nn.sigmoid(g) * u).astype(BF16)
        x2 = x2 + jnp.dot(act, wd_ref[lo:hi, :], preferred_element_type=F32)
    lo, hi = FFN_COL_CHUNKS[-1]
    g = jnp.dot(h, wg_ref[:, lo:hi], preferred_element_type=F32)
    u = jnp.dot(h, wu_ref[:, lo:hi], preferred_element_type=F32)
    act = (g * jax.nn.sigmoid(g) * u).astype(BF16)
    piece = tile // FFN_TAIL_PIECES
    for r in range(FFN_TAIL_PIECES):
        rows = slice(r * piece, (r + 1) * piece)
        y = x2[rows, :] + jnp.dot(act[rows, :], wd_ref[lo:hi, :], preferred_element_type=F32)
        out_ref[0, rows, :] = _rms(y, gl_ref[...])


def _ffn_call(x, o_moba, o_dil, moba_out_norm, dil_out_norm, w_out, ffn_norm,
              w_gate, w_up, w_down, final_norm):
    B, S, D = x.shape
    tile = TOKEN_TILE
    half = N_HEADS_MOBA * HEAD_DIM

    def const(shape):
        return pl.BlockSpec(shape, lambda b, s: (0,) * len(shape), pipeline_mode=pl.Buffered(1))

    row_spec = pl.BlockSpec((1, tile, D), lambda b, s: (b, s, 0))
    head_spec = pl.BlockSpec((1, N_HEADS_MOBA, tile, LANES), lambda b, s: (b, 0, s, 0))
    return pl.pallas_call(
        _ffn_kernel,
        grid=(B, S // tile),
        in_specs=[row_spec, head_spec, head_spec,
                  const((1, half)), const((1, half)), const((D, D)), const((1, D)),
                  const((D, D_FF)), const((D, D_FF)), const((D_FF, D)), const((1, D))],
        out_specs=row_spec,
        out_shape=jax.ShapeDtypeStruct((B, S, D), F32),
        compiler_params=pltpu.CompilerParams(
            dimension_semantics=("arbitrary", "arbitrary"),
            vmem_limit_bytes=VMEM_LIMIT_LARGE),
        name="outproj_ffn",
    )(x, o_moba, o_dil, moba_out_norm.reshape(1, half), dil_out_norm.reshape(1, half),
      w_out, ffn_norm.reshape(1, D), w_gate, w_up, w_down, final_norm.reshape(1, D))


def _rope_tables(S):
    inv_freq = ROPE_THETA ** (-np.arange(0, HEAD_DIM, 2, dtype=np.float64) / HEAD_DIM)
    ang = np.arange(S, dtype=np.float64)[:, None] * inv_freq[None, :]
    cos = np.concatenate([np.cos(ang)] * 4, axis=-1)
    sin = np.concatenate([-np.sin(ang), np.sin(ang)] * 2, axis=-1)
    return jnp.asarray(cos, F32), jnp.asarray(sin, F32)


def kernel(x, attn_norm, w_in, moba_out_norm, dil_out_norm, w_out, ffn_norm, w_gate, w_up,
           w_down, final_norm):
    B, S, D = x.shape
    assert DILATIONS == (1, 4, 16)
    assert D == D_MODEL and S % TOKEN_TILE == 0 and S % QKV_TILE == 0
    assert S % (DILATIONS[-1] * WINDOW_LEN) == 0
    assert S // MOBA_BLOCK <= LANES - ONEHOT_LANE0
    cos, sin = _rope_tables(S)
    v0 = 2 * D_MODEL
    wv_moba = lax.optimization_barrier(w_in[:, v0:v0 + N_HEADS_MOBA * HEAD_DIM])
    wv_moba_t = wv_moba.T.astype(BF16)
    qm, km, vmt, qd, kd, vd = _qkv_call(x, attn_norm, w_in.astype(BF16), wv_moba_t, cos, sin)
    o_moba = _moba_call(qm, km, vmt)
    o_dil = _dilated_call(qd, kd, vd)
    return _ffn_call(x, o_moba, o_dil, moba_out_norm, dil_out_norm, w_out.astype(BF16),
                     ffn_norm, w_gate.astype(BF16), w_up.astype(BF16), w_down.astype(BF16),
                     final_norm)
```

```python
import math

import jax
import jax.numpy as jnp
import numpy as np
from jax import lax
from jax.experimental import pallas as pl
from jax.experimental.pallas import tpu as pltpu

F32 = jnp.float32
BF16 = jnp.bfloat16

D_MODEL = 1024
HEAD_DIM = 64
N_HEADS = D_MODEL // HEAD_DIM
N_HEADS_MOBA = N_HEADS // 2
MOBA_BLOCK = 256
MOBA_TOPK = 3
DILATIONS = (1, 4, 16)
WINDOW_LEN = 128
D_FF = 2816
ROPE_THETA = 10000.0
RMS_EPS = 1e-6
NEG_INF = -1e30
ATTN_SCALE = 1.0 / math.sqrt(HEAD_DIM)
Q_SCALE = ATTN_SCALE * math.log2(math.e)

LANES = 128
F32_SUBLANES = 8
BF16_SUBLANES = 16
SUM_LANE = HEAD_DIM
VT_ROWS = HEAD_DIM + BF16_SUBLANES
ONEHOT_LANE0 = HEAD_DIM

QKV_TILE = 1024
TOKEN_TILE = 512
FFN_COL_CHUNKS = ((0, 1536), (1536, D_FF))
FFN_ROW_CHUNKS = 4
FFN_TAIL_PIECES = 2
MERGE_ROWS = 256
DILATED_GROUP = 48
MOBA_STEP = 4
GATE_CHUNK = 1024
VMEM_LIMIT = 56 * 1024 * 1024

_NT = (((1,), (1,)), ((), ()))
_TN = (((0,), (0,)), ((), ()))


def _interleave(*generators):
    running = list(generators)
    while running:
        for g in list(running):
            if next(g, StopIteration) is StopIteration:
                running.remove(g)


def _rms(y, gain):
    return y * lax.rsqrt(jnp.mean(y * y, axis=-1, keepdims=True) + RMS_EPS) * gain


def _qkv_kernel(x_ref, g_ref, w_ref, wvt_ref, cos_ref, sin_ref,
                qm_ref, km_ref, vmt_ref, qd_ref, kd_ref, vd_ref):
    tile = x_ref.shape[1]
    moba_width = N_HEADS_MOBA * HEAD_DIM
    h = _rms(x_ref[0], g_ref[...]).astype(BF16)
    cos = cos_ref[...]
    sin = sin_ref[...]
    lane = lax.broadcasted_iota(jnp.int32, (tile, LANES), 1)
    row = lax.broadcasted_iota(jnp.int32, (tile, LANES), 0)
    low_head = lane < HEAD_DIM
    first_half = (lane & (HEAD_DIM - 1)) < HEAD_DIM // 2
    block_of_row = (pl.program_id(1) * tile + row) // MOBA_BLOCK
    onehot = jnp.where(lane - ONEHOT_LANE0 == block_of_row, 1.0, 0.0)
    ones_col = jnp.where(lane == SUM_LANE, 1.0, 0.0)
    zeros = jnp.zeros((tile, LANES), F32)

    def rope(y):
        rot = jnp.where(first_half, pltpu.roll(y, LANES - HEAD_DIM // 2, 1),
                        pltpu.roll(y, HEAD_DIM // 2, 1))
        return y * cos + rot * sin

    def split(y, pad):
        return (jnp.where(low_head, y, pad),
                jnp.where(low_head, pltpu.roll(y, HEAD_DIM, 1), pad))

    q = jnp.dot(h, w_ref[:, 0:D_MODEL], preferred_element_type=F32)
    k = jnp.dot(h, w_ref[:, D_MODEL:2 * D_MODEL], preferred_element_type=F32)
    v = jnp.dot(h, w_ref[:, 2 * D_MODEL + moba_width:3 * D_MODEL], preferred_element_type=F32)
    for pair in range(N_HEADS // 2):
        cols = slice(pair * LANES, (pair + 1) * LANES)
        moba = pair < N_HEADS_MOBA // 2
        q_ref, k_ref = (qm_ref, km_ref) if moba else (qd_ref, kd_ref)
        head0 = 2 * pair - (0 if moba else N_HEADS_MOBA)
        qs = split(rope(q[:, cols]) * Q_SCALE, zeros)
        ks = split(rope(k[:, cols]), onehot if moba else zeros)
        for a in range(2):
            q_ref[0, head0 + a] = qs[a].astype(BF16)
            k_ref[0, head0 + a] = ks[a].astype(BF16)
        if not moba:
            vs = split(v[:, head0 * HEAD_DIM:head0 * HEAD_DIM + LANES], ones_col)
            for a in range(2):
                vd_ref[0, head0 + a] = vs[a].astype(BF16)

    vt = lax.dot_general(wvt_ref[...], h, _NT, preferred_element_type=F32)
    sub = lax.broadcasted_iota(jnp.int32, (VT_ROWS - HEAD_DIM, tile), 0)
    ones_row = jnp.where(sub == SUM_LANE - HEAD_DIM, 1.0, 0.0)
    for head in range(N_HEADS_MOBA):
        slab = jnp.concatenate([vt[head * HEAD_DIM:(head + 1) * HEAD_DIM, :], ones_row],
                               axis=0).astype(BF16)
        for c in range(tile // MOBA_BLOCK):
            vmt_ref[0, head, c] = slab[:, c * MOBA_BLOCK:(c + 1) * MOBA_BLOCK]


def _qkv_call(x, attn_norm, w_in, wv_moba_t, cos, sin):
    B, S, D = x.shape
    tile = QKV_TILE
    H = N_HEADS_MOBA
    head_shape = jax.ShapeDtypeStruct((B, H, S, LANES), BF16)
    head_spec = pl.BlockSpec((1, H, tile, LANES), lambda b, s: (b, 0, s, 0))
    vt_shape = jax.ShapeDtypeStruct((B, H, S // MOBA_BLOCK, VT_ROWS, MOBA_BLOCK), BF16)
    vt_spec = pl.BlockSpec((1, H, tile // MOBA_BLOCK, VT_ROWS, MOBA_BLOCK),
                           lambda b, s: (b, 0, s, 0, 0))
    return pl.pallas_call(
        _qkv_kernel,
        grid=(B, S // tile),
        in_specs=[
            pl.BlockSpec((1, tile, D), lambda b, s: (b, s, 0)),
            pl.BlockSpec((1, D), lambda b, s: (0, 0)),
            pl.BlockSpec((D, 3 * D), lambda b, s: (0, 0), pipeline_mode=pl.Buffered(1)),
            pl.BlockSpec(wv_moba_t.shape, lambda b, s: (0, 0), pipeline_mode=pl.Buffered(1)),
            pl.BlockSpec((tile, LANES), lambda b, s: (s, 0)),
            pl.BlockSpec((tile, LANES), lambda b, s: (s, 0)),
        ],
        out_specs=[head_spec, head_spec, vt_spec, head_spec, head_spec, head_spec],
        out_shape=[head_shape, head_shape, vt_shape, head_shape, head_shape, head_shape],
        compiler_params=pltpu.CompilerParams(
            dimension_semantics=("arbitrary", "arbitrary"),
            vmem_limit_bytes=VMEM_LIMIT),
        name="qkv_rope",
    )(x, attn_norm.reshape(1, D), w_in, wv_moba_t, cos, sin)


def _moba_kernel(q_ref, k_ref, vt_ref, o_ref, qb_ref, s_ref, max_ref):
    blk = MOBA_BLOCK
    S = k_ref.shape[2]
    n_blocks = S // blk

    kmean = jnp.concatenate(
        [jnp.mean(k_ref[0, 0, n * blk:(n + 1) * blk, :].astype(F32), axis=0, keepdims=True)
         for n in range(n_blocks)], axis=0)
    sub = lax.broadcasted_iota(jnp.int32, (n_blocks, LANES), 0)
    lane = lax.broadcasted_iota(jnp.int32, (n_blocks, LANES), 1)
    place = jnp.where(lane == sub + ONEHOT_LANE0, 1.0, 0.0).astype(BF16)
    k_hi = kmean.astype(BF16)
    k_mid = (kmean - k_hi.astype(F32)).astype(BF16)
    k_lo = (kmean - k_hi.astype(F32) - k_mid.astype(F32)).astype(BF16)
    k_terms = jnp.concatenate([k_hi, k_mid, k_lo], axis=0)
    gc = GATE_CHUNK
    n_idx = lax.broadcasted_iota(jnp.int32, (n_blocks, gc), 0)
    pos = lax.broadcasted_iota(jnp.int32, (n_blocks, gc), 1)
    q_lanes = lax.broadcasted_iota(jnp.int32, (gc, LANES), 1) < HEAD_DIM
    for c in range(S // gc):
        q = q_ref[0, 0, c * gc:(c + 1) * gc, :]
        g3 = lax.dot_general(k_terms, q, _NT, preferred_element_type=F32)
        gate_t = g3[:n_blocks] + g3[n_blocks:2 * n_blocks] + g3[2 * n_blocks:]
        own = (pos + c * gc) // blk
        past = n_idx < own
        gate_t = jnp.where(past, gate_t, NEG_INF)
        chosen = jnp.zeros((n_blocks, gc), F32)
        for _ in range(MOBA_TOPK):
            best = jnp.max(gate_t, axis=0, keepdims=True)
            first = jnp.min(jnp.where(gate_t == best, n_idx, n_blocks), axis=0, keepdims=True)
            hit = n_idx == first
            chosen = jnp.where(hit, 1.0, chosen)
            gate_t = jnp.where(hit, -jnp.inf, gate_t)
        visible = ((chosen > 0.0) & past) | (n_idx == own)
        bias_t = jnp.where(visible, 0.0, NEG_INF).astype(BF16)
        bias = lax.dot_general(bias_t, place, _TN, preferred_element_type=F32)
        qb_ref[c * gc:(c + 1) * gc, :] = jnp.where(q_lanes, q, bias.astype(BF16))

    def score_pass(own):
        qb = qb_ref[own * blk:(own + 1) * blk, :]
        m8 = None
        for j in range(own + 1):
            s = lax.dot_general(k_ref[0, 0, j * blk:(j + 1) * blk, :], qb, _NT,
                                preferred_element_type=F32)
            if j == own:
                key_idx = lax.broadcasted_iota(jnp.int32, (blk, blk), 0)
                qry_idx = lax.broadcasted_iota(jnp.int32, (blk, blk), 1)
                s = jnp.where(key_idx <= qry_idx, s, NEG_INF)
            s_ref[own % (2 * MOBA_STEP), j * blk:(j + 1) * blk, :] = s
            s8 = jnp.max(s.reshape(blk // F32_SUBLANES, F32_SUBLANES, blk), axis=0)
            m8 = s8 if m8 is None else jnp.maximum(m8, s8)
            yield
        max_ref[own:own + 1, :] = jnp.max(m8, axis=0, keepdims=True)

    def value_pass(own):
        m = max_ref[own:own + 1, :]
        acc_t = None
        for j in range(own + 1):
            p = jnp.exp2(s_ref[own % (2 * MOBA_STEP), j * blk:(j + 1) * blk, :] - m).astype(BF16)
            part = jnp.dot(vt_ref[0, 0, j], p, preferred_element_type=F32)
            acc_t = part if acc_t is None else acc_t + part
            yield
        out_t = acc_t / acc_t[SUM_LANE:SUM_LANE + 1, :]
        out_t = jnp.concatenate([out_t, jnp.zeros((LANES - VT_ROWS, blk), F32)], axis=0)
        o_ref[0, 0, own * blk:(own + 1) * blk, :] = out_t.T.astype(BF16)

    n_steps = n_blocks // MOBA_STEP
    for t in range(n_steps + 1):
        @pl.when(pl.program_id(0) > -1 - t)
        def _(t=t):
            parts = []
            for i in range(MOBA_STEP):
                if t > 0:
                    parts.append(value_pass((t - 1) * MOBA_STEP + i))
                if t < n_steps:
                    parts.append(score_pass(t * MOBA_STEP + i))
            _interleave(*parts)


def _moba_call(q, k, vt):
    B, H, S, _ = q.shape
    blk = MOBA_BLOCK
    assert S % GATE_CHUNK == 0
    seq_spec = pl.BlockSpec((1, 1, S, LANES), lambda b, h: (b, h, 0, 0))
    vt_spec = pl.BlockSpec((1, 1, S // blk, VT_ROWS, blk), lambda b, h: (b, h, 0, 0, 0))
    return pl.pallas_call(
        _moba_kernel,
        grid=(B, H),
        in_specs=[seq_spec, seq_spec, vt_spec],
        out_specs=seq_spec,
        out_shape=jax.ShapeDtypeStruct((B, H, S, LANES), BF16),
        scratch_shapes=[pltpu.VMEM((S, LANES), BF16),
                        pltpu.VMEM((2 * MOBA_STEP, S, blk), F32),
                        pltpu.VMEM((S // blk, blk), F32)],
        compiler_params=pltpu.CompilerParams(
            dimension_semantics=("arbitrary", "arbitrary"),
            vmem_limit_bytes=VMEM_LIMIT),
        name="moba_attention",
    )(q, k, vt)


def _dilated_kernel(q_ref, k_ref, v_ref, o_ref, stage_ref, stage4_ref,
                    q4_ref, k4_ref, v4_ref, q16_ref, k16_ref, v16_ref,
                    acc1_ref, acc4_ref, acc16_ref, max1_ref, max4_ref, max16_ref,
                    s_ref, m_ref):
    S = q_ref.shape[2]
    L = WINDOW_LEN
    q1_ref, k1_ref, v1_ref = q_ref.at[0, 0], k_ref.at[0, 0], v_ref.at[0, 0]

    n4, n16 = S // 4, S // 16

    def deinterleave(src, dst4, dst16):
        stage_ref[...] = src[...].astype(F32)
        for r in range(4):
            x = stage_ref[pl.ds(r, n4, stride=4), :]
            dst4[r * n4:(r + 1) * n4, :] = x.astype(BF16)
            stage4_ref[r * n4:(r + 1) * n4, :] = x
        for r in range(16):
            x = stage4_ref[pl.ds((r % 4) * n4 + r // 4, n16, stride=4), :]
            dst16[r * n16:(r + 1) * n16, :] = x.astype(BF16)
        yield

    acc16t_ref, max16t_ref = stage_ref, stage4_ref

    def order16():
        for r in range(4):
            acc16_ref[pl.ds(r, n4, stride=4), :] = acc16t_ref[r * n4:(r + 1) * n4, :]
            max16_ref[pl.ds(r, n4, stride=4), :] = max16t_ref[r * n4:(r + 1) * n4, :]
        yield

    tiles = []
    for d, qd_ref, kd_ref, vd_ref, acc_ref, max_ref in (
            (1, q1_ref, k1_ref, v1_ref, acc1_ref, max1_ref),
            (4, q4_ref, k4_ref, v4_ref, acc4_ref, max4_ref),
            (16, q16_ref, k16_ref, v16_ref, acc16t_ref, max16t_ref)):
        n = S // d
        for residue in range(d):
            for t in range(n // L):
                base = residue * n + t * L
                keys = slice(base - L if t > 0 else base, base + L)
                if d == 1:
                    rows = slice(base, base + L)
                elif d == 4:
                    rows = pl.ds(t * L * 4 + residue, L, stride=4)
                else:
                    rows = pl.ds((residue % 4) * n4 + t * L * 4 + residue // 4, L, stride=4)
                tiles.append((qd_ref, kd_ref, vd_ref, acc_ref, max_ref, base, keys, rows))
    group = DILATED_GROUP
    n_groups = len(tiles) // group

    def score_pass(g):
        for i, (qd_ref, kd_ref, _, _, max_ref, base, keys, rows) in enumerate(
                tiles[g * group:(g + 1) * group]):
            width = keys.stop - keys.start
            s = lax.dot_general(qd_ref[base:base + L, :], kd_ref[keys, :], _NT,
                                preferred_element_type=F32)
            r_idx = lax.broadcasted_iota(jnp.int32, (L, width), 0)
            c_idx = lax.broadcasted_iota(jnp.int32, (L, width), 1)
            if width == 2 * L:
                seen = ((c_idx < L) & (c_idx >= r_idx)) | ((c_idx >= L) & (c_idx - L <= r_idx))
            else:
                seen = c_idx <= r_idx
            s = jnp.where(seen, s, NEG_INF)
            m = jnp.max(s, axis=1, keepdims=True)
            m = jnp.broadcast_to(m, (L, LANES))
            s_ref[g % 2, i * L:(i + 1) * L, 0:width] = s
            m_ref[g % 2, i * L:(i + 1) * L, :] = m
            max_ref[rows, :] = m
            yield

    def value_pass(g):
        for i, (_, _, vd_ref, acc_ref, _, _, keys, rows) in enumerate(
                tiles[g * group:(g + 1) * group]):
            width = keys.stop - keys.start
            m = m_ref[g % 2, i * L:(i + 1) * L, :]
            p = jnp.concatenate(
                [jnp.exp2(s_ref[g % 2, i * L:(i + 1) * L, c * L:(c + 1) * L] - m)
                 for c in range(width // L)], axis=1).astype(BF16)
            acc_ref[rows, :] = jnp.dot(p, vd_ref[keys, :], preferred_element_type=F32)
            yield

    def merge():
        for c in range(S // MERGE_ROWS):
            rows = slice(c * MERGE_ROWS, (c + 1) * MERGE_ROWS)
            maxes = [ref[rows, :] for ref in (max1_ref, max4_ref, max16_ref)]
            m = jnp.maximum(jnp.maximum(maxes[0], maxes[1]), maxes[2])
            total = sum(jnp.exp2(mi - m) * ref[rows, :]
                        for mi, ref in zip(maxes, (acc1_ref, acc4_ref, acc16_ref)))
            o_ref[0, 0, rows, :] = (total / total[:, SUM_LANE:SUM_LANE + 1]).astype(BF16)
            yield

    copies = [(q1_ref, q4_ref, q16_ref), (k1_ref, k4_ref, k16_ref), (v1_ref, v4_ref, v16_ref)]
    first_dilated = (S // L) // group
    copy_step = [0, max(0, min(1, first_dilated - 1)), min(2, max(first_dilated, 1))]
    for t in range(n_groups + 2):
        @pl.when(pl.program_id(0) > -1 - t)
        def _(t=t):
            parts = [deinterleave(*c) for c, at in zip(copies, copy_step) if at == t]
            if 0 < t <= n_groups:
                parts.append(value_pass(t - 1))
            if t < n_groups:
                parts.append(score_pass(t))
            if t == n_groups + 1:
                _interleave(order16())
                parts.append(merge())
            _interleave(*parts)


def _dilated_call(q, k, v):
    B, H, S, _ = q.shape
    seq_spec = pl.BlockSpec((1, 1, S, LANES), lambda b, h: (b, h, 0, 0))
    return pl.pallas_call(
        _dilated_kernel,
        grid=(B, H),
        in_specs=[seq_spec] * 3,
        out_specs=seq_spec,
        out_shape=jax.ShapeDtypeStruct((B, H, S, LANES), BF16),
        scratch_shapes=([pltpu.VMEM((S, LANES), F32)] * 2
                        + [pltpu.VMEM((S, LANES), BF16)] * 6
                        + [pltpu.VMEM((S, LANES), F32)] * 6
                        + [pltpu.VMEM((2, DILATED_GROUP * WINDOW_LEN, 2 * WINDOW_LEN), F32),
                           pltpu.VMEM((2, DILATED_GROUP * WINDOW_LEN, LANES), F32)]),
        compiler_params=pltpu.CompilerParams(
            dimension_semantics=("arbitrary", "arbitrary"),
            vmem_limit_bytes=VMEM_LIMIT),
        name="dilated_attention",
    )(q, k, v)


def _ffn_kernel(x_ref, om_ref, od_ref, gm_ref, gd_ref, wo_ref, gf_ref,
                wg_ref, wu_ref, wd_ref, gl_ref, out_ref):
    tile = x_ref.shape[1]
    sub = tile // FFN_ROW_CHUNKS
    lane = lax.broadcasted_iota(jnp.int32, (sub, LANES), 1)
    low_head = lane < HEAD_DIM

    def heads(ref, rows):
        pairs = []
        for p in range(ref.shape[1] // 2):
            a = ref[0, 2 * p, rows, :].astype(F32)
            b = ref[0, 2 * p + 1, rows, :].astype(F32)
            pairs.append(jnp.where(low_head, a, pltpu.roll(b, HEAD_DIM, 1)))
        return jnp.concatenate(pairs, axis=1)

    x1_parts, h_parts = [], []
    for r in range(FFN_ROW_CHUNKS):
        rows = slice(r * sub, (r + 1) * sub)
        mixed = jnp.concatenate([_rms(heads(om_ref, rows), gm_ref[...]),
                                 _rms(heads(od_ref, rows), gd_ref[...])], axis=1).astype(BF16)
        x1 = x_ref[0, rows, :] + jnp.dot(mixed, wo_ref[...], preferred_element_type=F32)
        x1_parts.append(x1)
        h_parts.append(_rms(x1, gf_ref[...]).astype(BF16))
    x2 = jnp.concatenate(x1_parts, axis=0)
    h = jnp.concatenate(h_parts, axis=0)
    for lo, hi in FFN_COL_CHUNKS[:-1]:
        g = jnp.dot(h, wg_ref[:, lo:hi], preferred_element_type=F32)
        u = jnp.dot(h, wu_ref[:, lo:hi], preferred_element_type=F32)
        act = (g * jax.nn.sigmoid(g) * u).astype(BF16)
        x2 = x2 + jnp.dot(act, wd_ref[lo:hi, :], preferred_element_type=F32)
    lo, hi = FFN_COL_CHUNKS[-1]
    g = jnp.dot(h, wg_ref[:, lo:hi], preferred_element_type=F32)
    u = jnp.dot(h, wu_ref[:, lo:hi], preferred_element_type=F32)
    act = (g * jax.nn.sigmoid(g) * u).astype(BF16)
    piece = tile // FFN_TAIL_PIECES
    for r in range(FFN_TAIL_PIECES):
        rows = slice(r * piece, (r + 1) * piece)
        y = x2[rows, :] + jnp.dot(act[rows, :], wd_ref[lo:hi, :], preferred_element_type=F32)
        out_ref[0, rows, :] = _rms(y, gl_ref[...])


def _ffn_call(x, o_moba, o_dil, moba_out_norm, dil_out_norm, w_out, ffn_norm,
              w_gate, w_up, w_down, final_norm):
    B, S, D = x.shape
    tile = TOKEN_TILE
    half = N_HEADS_MOBA * HEAD_DIM

    def const(shape):
        return pl.BlockSpec(shape, lambda b, s: (0,) * len(shape), pipeline_mode=pl.Buffered(1))

    row_spec = pl.BlockSpec((1, tile, D), lambda b, s: (b, s, 0))
    head_spec = pl.BlockSpec((1, N_HEADS_MOBA, tile, LANES), lambda b, s: (b, 0, s, 0))
    return pl.pallas_call(
        _ffn_kernel,
        grid=(B, S // tile),
        in_specs=[row_spec, head_spec, head_spec,
                  const((1, half)), const((1, half)), const((D, D)), const((1, D)),
                  const((D, D_FF)), const((D, D_FF)), const((D_FF, D)), const((1, D))],
        out_specs=row_spec,
        out_shape=jax.ShapeDtypeStruct((B, S, D), F32),
        compiler_params=pltpu.CompilerParams(
            dimension_semantics=("arbitrary", "arbitrary"),
            vmem_limit_bytes=VMEM_LIMIT),
        name="outproj_ffn",
    )(x, o_moba, o_dil, moba_out_norm.reshape(1, half), dil_out_norm.reshape(1, half),
      w_out, ffn_norm.reshape(1, D), w_gate, w_up, w_down, final_norm.reshape(1, D))


def _rope_tables(S):
    inv_freq = ROPE_THETA ** (-np.arange(0, HEAD_DIM, 2, dtype=np.float64) / HEAD_DIM)
    ang = np.arange(S, dtype=np.float64)[:, None] * inv_freq[None, :]
    cos = np.concatenate([np.cos(ang)] * 4, axis=-1)
    sin = np.concatenate([-np.sin(ang), np.sin(ang)] * 2, axis=-1)
    return jnp.asarray(cos, F32), jnp.asarray(sin, F32)


def kernel(x, attn_norm, w_in, moba_out_norm, dil_out_norm, w_out, ffn_norm, w_gate, w_up,
           w_down, final_norm):
    B, S, D = x.shape
    assert DILATIONS == (1, 4, 16)
    assert D == D_MODEL and S % TOKEN_TILE == 0 and S % QKV_TILE == 0
    assert S % (DILATIONS[-1] * WINDOW_LEN) == 0
    assert S // MOBA_BLOCK <= LANES - ONEHOT_LANE0
    cos, sin = _rope_tables(S)
    v0 = 2 * D_MODEL
    wv_moba = lax.optimization_barrier(w_in[:, v0:v0 + N_HEADS_MOBA * HEAD_DIM])
    wv_moba_t = wv_moba.T.astype(BF16)
    qm, km, vmt, qd, kd, vd = _qkv_call(x, attn_norm, w_in.astype(BF16), wv_moba_t, cos, sin)
    o_moba = _moba_call(qm, km, vmt)
    o_dil = _dilated_call(qd, kd, vd)
    return _ffn_call(x, o_moba, o_dil, moba_out_norm, dil_out_norm, w_out.astype(BF16),
                     ffn_norm, w_gate.astype(BF16), w_up.astype(BF16), w_down.astype(BF16),
                     final_norm)
```

```python
import math

import jax
import jax.numpy as jnp
import numpy as np
from jax import lax
from jax.experimental import pallas as pl
from jax.experimental.pallas import tpu as pltpu

F32 = jnp.float32
BF16 = jnp.bfloat16

D_MODEL = 1024
HEAD_DIM = 64
N_HEADS = D_MODEL // HEAD_DIM
N_HEADS_MOBA = N_HEADS // 2
MOBA_BLOCK = 256
MOBA_TOPK = 3
DILATIONS = (1, 4, 16)
WINDOW_LEN = 128
D_FF = 2816
ROPE_THETA = 10000.0
RMS_EPS = 1e-6
NEG_INF = -1e30
ATTN_SCALE = 1.0 / math.sqrt(HEAD_DIM)
Q_SCALE = ATTN_SCALE * math.log2(math.e)

LANES = 128
F32_SUBLANES = 8
BF16_SUBLANES = 16
SUM_LANE = HEAD_DIM
VT_ROWS = HEAD_DIM + BF16_SUBLANES
ONEHOT_LANE0 = HEAD_DIM

QKV_TILE = 1024
TOKEN_TILE = 512
FFN_COL_CHUNKS = ((0, 1536), (1536, D_FF))
FFN_ROW_CHUNKS = 4
FFN_TAIL_PIECES = 2
MERGE_ROWS = 256
DILATED_GROUP = 48
MOBA_STEP = 4
GATE_CHUNK = 1024
VMEM_LIMIT = 56 * 1024 * 1024

_NT = (((1,), (1,)), ((), ()))
_TN = (((0,), (0,)), ((), ()))


def _interleave(*generators):
    running = list(generators)
    while running:
        for g in list(running):
            if next(g, StopIteration) is StopIteration:
                running.remove(g)


def _rms(y, gain):
    return y * lax.rsqrt(jnp.mean(y * y, axis=-1, keepdims=True) + RMS_EPS) * gain


def _qkv_kernel(x_ref, g_ref, w_ref, wvt_ref, cos_ref, sin_ref,
                qm_ref, km_ref, vmt_ref, qd_ref, kd_ref, vd_ref):
    tile = x_ref.shape[1]
    moba_width = N_HEADS_MOBA * HEAD_DIM
    h = _rms(x_ref[0], g_ref[...]).astype(BF16)
    cos = cos_ref[...]
    sin = sin_ref[...]
    lane = lax.broadcasted_iota(jnp.int32, (tile, LANES), 1)
    row = lax.broadcasted_iota(jnp.int32, (tile, LANES), 0)
    low_head = lane < HEAD_DIM
    first_half = (lane & (HEAD_DIM - 1)) < HEAD_DIM // 2
    block_of_row = (pl.program_id(1) * tile + row) // MOBA_BLOCK
    onehot = jnp.where(lane - ONEHOT_LANE0 == block_of_row, 1.0, 0.0)
    ones_col = jnp.where(lane == SUM_LANE, 1.0, 0.0)
    zeros = jnp.zeros((tile, LANES), F32)

    def rope(y):
        rot = jnp.where(first_half, pltpu.roll(y, LANES - HEAD_DIM // 2, 1),
                        pltpu.roll(y, HEAD_DIM // 2, 1))
        return y * cos + rot * sin

    def split(y, pad):
        return (jnp.where(low_head, y, pad),
                jnp.where(low_head, pltpu.roll(y, HEAD_DIM, 1), pad))

    q = jnp.dot(h, w_ref[:, 0:D_MODEL], preferred_element_type=F32)
    k = jnp.dot(h, w_ref[:, D_MODEL:2 * D_MODEL], preferred_element_type=F32)
    v = jnp.dot(h, w_ref[:, 2 * D_MODEL + moba_width:3 * D_MODEL], preferred_element_type=F32)
    for pair in range(N_HEADS // 2):
        cols = slice(pair * LANES, (pair + 1) * LANES)
        moba = pair < N_HEADS_MOBA // 2
        q_ref, k_ref = (qm_ref, km_ref) if moba else (qd_ref, kd_ref)
        head0 = 2 * pair - (0 if moba else N_HEADS_MOBA)
        qs = split(rope(q[:, cols]) * Q_SCALE, zeros)
        ks = split(rope(k[:, cols]), onehot if moba else zeros)
        for a in range(2):
            q_ref[0, head0 + a] = qs[a].astype(BF16)
            k_ref[0, head0 + a] = ks[a].astype(BF16)
        if not moba:
            vs = split(v[:, head0 * HEAD_DIM:head0 * HEAD_DIM + LANES], ones_col)
            for a in range(2):
                vd_ref[0, head0 + a] = vs[a].astype(BF16)

    vt = lax.dot_general(wvt_ref[...], h, _NT, preferred_element_type=F32)
    sub = lax.broadcasted_iota(jnp.int32, (VT_ROWS - HEAD_DIM, tile), 0)
    ones_row = jnp.where(sub == SUM_LANE - HEAD_DIM, 1.0, 0.0)
    for head in range(N_HEADS_MOBA):
        slab = jnp.concatenate([vt[head * HEAD_DIM:(head + 1) * HEAD_DIM, :], ones_row],
                               axis=0).astype(BF16)
        for c in range(tile // MOBA_BLOCK):
            vmt_ref[0, head, c] = slab[:, c * MOBA_BLOCK:(c + 1) * MOBA_BLOCK]


def _qkv_call(x, attn_norm, w_in, wv_moba_t, cos, sin):
    B, S, D = x.shape
    tile = QKV_TILE
    H = N_HEADS_MOBA
    head_shape = jax.ShapeDtypeStruct((B, H, S, LANES), BF16)
    head_spec = pl.BlockSpec((1, H, tile, LANES), lambda b, s: (b, 0, s, 0))
    vt_shape = jax.ShapeDtypeStruct((B, H, S // MOBA_BLOCK, VT_ROWS, MOBA_BLOCK), BF16)
    vt_spec = pl.BlockSpec((1, H, tile // MOBA_BLOCK, VT_ROWS, MOBA_BLOCK),
                           lambda b, s: (b, 0, s, 0, 0))
    return pl.pallas_call(
        _qkv_kernel,
        grid=(B, S // tile),
        in_specs=[
            pl.BlockSpec((1, tile, D), lambda b, s: (b, s, 0)),
            pl.BlockSpec((1, D), lambda b, s: (0, 0)),
            pl.BlockSpec((D, 3 * D), lambda b, s: (0, 0), pipeline_mode=pl.Buffered(1)),
            pl.BlockSpec(wv_moba_t.shape, lambda b, s: (0, 0), pipeline_mode=pl.Buffered(1)),
            pl.BlockSpec((tile, LANES), lambda b, s: (s, 0)),
            pl.BlockSpec((tile, LANES), lambda b, s: (s, 0)),
        ],
        out_specs=[head_spec, head_spec, vt_spec, head_spec, head_spec, head_spec],
        out_shape=[head_shape, head_shape, vt_shape, head_shape, head_shape, head_shape],
        compiler_params=pltpu.CompilerParams(
            dimension_semantics=("arbitrary", "arbitrary"),
            vmem_limit_bytes=VMEM_LIMIT),
        name="qkv_rope",
    )(x, attn_norm.reshape(1, D), w_in, wv_moba_t, cos, sin)


def _moba_kernel(q_ref, k_ref, vt_ref, o_ref, qb_ref, s_ref, max_ref):
    blk = MOBA_BLOCK
    S = k_ref.shape[2]
    n_blocks = S // blk

    kmean = jnp.concatenate(
        [jnp.mean(k_ref[0, 0, n * blk:(n + 1) * blk, :].astype(F32), axis=0, keepdims=True)
         for n in range(n_blocks)], axis=0)
    sub = lax.broadcasted_iota(jnp.int32, (n_blocks, LANES), 0)
    lane = lax.broadcasted_iota(jnp.int32, (n_blocks, LANES), 1)
    place = jnp.where(lane == sub + ONEHOT_LANE0, 1.0, 0.0).astype(BF16)
    k_hi = kmean.astype(BF16)
    k_mid = (kmean - k_hi.astype(F32)).astype(BF16)
    k_lo = (kmean - k_hi.astype(F32) - k_mid.astype(F32)).astype(BF16)
    k_terms = jnp.concatenate([k_hi, k_mid, k_lo], axis=0)
    gc = GATE_CHUNK
    n_idx = lax.broadcasted_iota(jnp.int32, (n_blocks, gc), 0)
    pos = lax.broadcasted_iota(jnp.int32, (n_blocks, gc), 1)
    q_lanes = lax.broadcasted_iota(jnp.int32, (gc, LANES), 1) < HEAD_DIM
    for c in range(S // gc):
        q = q_ref[0, 0, c * gc:(c + 1) * gc, :]
        g3 = lax.dot_general(k_terms, q, _NT, preferred_element_type=F32)
        gate_t = g3[:n_blocks] + g3[n_blocks:2 * n_blocks] + g3[2 * n_blocks:]
        own = (pos + c * gc) // blk
        past = n_idx < own
        gate_t = jnp.where(past, gate_t, NEG_INF)
        chosen = jnp.zeros((n_blocks, gc), F32)
        for _ in range(MOBA_TOPK):
            best = jnp.max(gate_t, axis=0, keepdims=True)
            first = jnp.min(jnp.where(gate_t == best, n_idx, n_blocks), axis=0, keepdims=True)
            hit = n_idx == first
            chosen = jnp.where(hit, 1.0, chosen)
            gate_t = jnp.where(hit, -jnp.inf, gate_t)
        visible = ((chosen > 0.0) & past) | (n_idx == own)
        bias_t = jnp.where(visible, 0.0, NEG_INF).astype(BF16)
        bias = lax.dot_general(bias_t, place, _TN, preferred_element_type=F32)
        qb_ref[c * gc:(c + 1) * gc, :] = jnp.where(q_lanes, q, bias.astype(BF16))

    def score_pass(own):
        qb = qb_ref[own * blk:(own + 1) * blk, :]
        m8 = None
        for j in range(own + 1):
            s = lax.dot_general(k_ref[0, 0, j * blk:(j + 1) * blk, :], qb, _NT,
                                preferred_element_type=F32)
            if j == own:
                key_idx = lax.broadcasted_iota(jnp.int32, (blk, blk), 0)
                qry_idx = lax.broadcasted_iota(jnp.int32, (blk, blk), 1)
                s = jnp.where(key_idx <= qry_idx, s, NEG_INF)
            s_ref[own % (2 * MOBA_STEP), j * blk:(j + 1) * blk, :] = s
            s8 = jnp.max(s.reshape(blk // F32_SUBLANES, F32_SUBLANES, blk), axis=0)
            m8 = s8 if m8 is None else jnp.maximum(m8, s8)
            yield
        max_ref[own:own + 1, :] = jnp.max(m8, axis=0, keepdims=True)

    def value_pass(own):
        m = max_ref[own:own + 1, :]
        acc_t = None
        for j in range(own + 1):
            p = jnp.exp2(s_ref[own % (2 * MOBA_STEP), j * blk:(j + 1) * blk, :] - m).astype(BF16)
            part = jnp.dot(vt_ref[0, 0, j], p, preferred_element_type=F32)
            acc_t = part if acc_t is None else acc_t + part
            yield
        out_t = acc_t / acc_t[SUM_LANE:SUM_LANE + 1, :]
        out_t = jnp.concatenate([out_t, jnp.zeros((LANES - VT_ROWS, blk), F32)], axis=0)
        o_ref[0, 0, own * blk:(own + 1) * blk, :] = out_t.T.astype(BF16)

    n_steps = n_blocks // MOBA_STEP
    for t in range(n_steps + 1):
        @pl.when(pl.program_id(0) > -1 - t)
        def _(t=t):
            parts = []
            for i in range(MOBA_STEP):
                if t > 0:
                    parts.append(value_pass((t - 1) * MOBA_STEP + i))
                if t < n_steps:
                    parts.append(score_pass(t * MOBA_STEP + i))
            _interleave(*parts)


def _moba_call(q, k, vt):
    B, H, S, _ = q.shape
    blk = MOBA_BLOCK
    assert S % GATE_CHUNK == 0
    seq_spec = pl.BlockSpec((1, 1, S, LANES), lambda b, h: (b, h, 0, 0))
    vt_spec = pl.BlockSpec((1, 1, S // blk, VT_ROWS, blk), lambda b, h: (b, h, 0, 0, 0))
    return pl.pallas_call(
        _moba_kernel,
        grid=(B, H),
        in_specs=[seq_spec, seq_spec, vt_spec],
        out_specs=seq_spec,
        out_shape=jax.ShapeDtypeStruct((B, H, S, LANES), BF16),
        scratch_shapes=[pltpu.VMEM((S, LANES), BF16),
                        pltpu.VMEM((2 * MOBA_STEP, S, blk), F32),
                        pltpu.VMEM((S // blk, blk), F32)],
        compiler_params=pltpu.CompilerParams(
            dimension_semantics=("arbitrary", "arbitrary"),
            vmem_limit_bytes=VMEM_LIMIT),
        name="moba_attention",
    )(q, k, vt)


def _dilated_kernel(q_ref, k_ref, v_ref, o_ref, stage_ref, stage4_ref,
                    q4_ref, k4_ref, v4_ref, q16_ref, k16_ref, v16_ref,
                    acc1_ref, acc4_ref, acc16_ref, max1_ref, max4_ref, max16_ref,
                    s_ref):
    S = q_ref.shape[2]
    L = WINDOW_LEN
    q1_ref, k1_ref, v1_ref = q_ref.at[0, 0], k_ref.at[0, 0], v_ref.at[0, 0]

    n4, n16 = S // 4, S // 16

    def deinterleave(src, dst4, dst16):
        stage_ref[...] = src[...].astype(F32)
        for r in range(4):
            x = stage_ref[pl.ds(r, n4, stride=4), :]
            dst4[r * n4:(r + 1) * n4, :] = x.astype(BF16)
            stage4_ref[r * n4:(r + 1) * n4, :] = x
        for r in range(16):
            x = stage4_ref[pl.ds((r % 4) * n4 + r // 4, n16, stride=4), :]
            dst16[r * n16:(r + 1) * n16, :] = x.astype(BF16)
        yield

    acc16t_ref, max16t_ref = stage_ref, stage4_ref

    def order16():
        for r in range(4):
            acc16_ref[pl.ds(r, n4, stride=4), :] = acc16t_ref[r * n4:(r + 1) * n4, :]
            max16_ref[pl.ds(r, n4, stride=4), :] = max16t_ref[r * n4:(r + 1) * n4, :]
        yield

    tiles = []
    for d, qd_ref, kd_ref, vd_ref, acc_ref, max_ref in (
            (1, q1_ref, k1_ref, v1_ref, acc1_ref, max1_ref),
            (4, q4_ref, k4_ref, v4_ref, acc4_ref, max4_ref),
            (16, q16_ref, k16_ref, v16_ref, acc16t_ref, max16t_ref)):
        n = S // d
        for residue in range(d):
            for t in range(n // L):
                base = residue * n + t * L
                keys = slice(base - L if t > 0 else base, base + L)
                if d == 1:
                    rows = slice(base, base + L)
                elif d == 4:
                    rows = pl.ds(t * L * 4 + residue, L, stride=4)
                else:
                    rows = pl.ds((residue % 4) * n4 + t * L * 4 + residue // 4, L, stride=4)
                tiles.append((qd_ref, kd_ref, vd_ref, acc_ref, max_ref, base, keys, rows))
    group = DILATED_GROUP
    n_groups = len(tiles) // group

    def score_pass(g):
        for i, (qd_ref, kd_ref, _, _, max_ref, base, keys, rows) in enumerate(
                tiles[g * group:(g + 1) * group]):
            width = keys.stop - keys.start
            s = lax.dot_general(qd_ref[base:base + L, :], kd_ref[keys, :], _NT,
                                preferred_element_type=F32)
            r_idx = lax.broadcasted_iota(jnp.int32, (L, width), 0)
            c_idx = lax.broadcasted_iota(jnp.int32, (L, width), 1)
            if width == 2 * L:
                seen = ((c_idx < L) & (c_idx >= r_idx)) | ((c_idx >= L) & (c_idx - L <= r_idx))
            else:
                seen = c_idx <= r_idx
            s = jnp.where(seen, s, NEG_INF)
            m = jnp.max(s, axis=1, keepdims=True)
            m = jnp.broadcast_to(m, (L, LANES))
            s_ref[g % 2, i * L:(i + 1) * L, 0:width] = s
            max_ref[rows, :] = m
            yield

    def value_pass(g):
        for i, (_, _, vd_ref, acc_ref, max_ref, _, keys, rows) in enumerate(
                tiles[g * group:(g + 1) * group]):
            width = keys.stop - keys.start
            m = max_ref[rows, :]
            p = jnp.concatenate(
                [jnp.exp2(s_ref[g % 2, i * L:(i + 1) * L, c * L:(c + 1) * L] - m)
                 for c in range(width // L)], axis=1).astype(BF16)
            acc_ref[rows, :] = jnp.dot(p, vd_ref[keys, :], preferred_element_type=F32)
            yield

    def merge():
        for c in range(S // MERGE_ROWS):
            rows = slice(c * MERGE_ROWS, (c + 1) * MERGE_ROWS)
            maxes = [ref[rows, :] for ref in (max1_ref, max4_ref, max16_ref)]
            m = jnp.maximum(jnp.maximum(maxes[0], maxes[1]), maxes[2])
            total = sum(jnp.exp2(mi - m) * ref[rows, :]
                        for mi, ref in zip(maxes, (acc1_ref, acc4_ref, acc16_ref)))
            o_ref[0, 0, rows, :] = (total / total[:, SUM_LANE:SUM_LANE + 1]).astype(BF16)
            yield

    copies = [(q1_ref, q4_ref, q16_ref), (k1_ref, k4_ref, k16_ref), (v1_ref, v4_ref, v16_ref)]
    first_dilated = (S // L) // group
    copy_step = [0, max(0, min(1, first_dilated - 1)), min(2, max(first_dilated, 1))]
    for t in range(n_groups + 2):
        @pl.when(pl.program_id(0) > -1 - t)
        def _(t=t):
            parts = [deinterleave(*c) for c, at in zip(copies, copy_step) if at == t]
            if 0 < t <= n_groups:
                parts.append(value_pass(t - 1))
            if t < n_groups:
                parts.append(score_pass(t))
            if t == n_groups + 1:
                _interleave(order16())
                parts.append(merge())
            _interleave(*parts)


def _dilated_call(q, k, v):
    B, H, S, _ = q.shape
    seq_spec = pl.BlockSpec((1, 1, S, LANES), lambda b, h: (b, h, 0, 0))
    return pl.pallas_call(
        _dilated_kernel,
        grid=(B, H),
        in_specs=[seq_spec] * 3,
        out_specs=seq_spec,
        out_shape=jax.ShapeDtypeStruct((B, H, S, LANES), BF16),
        scratch_shapes=([pltpu.VMEM((S, LANES), F32)] * 2
                        + [pltpu.VMEM((S, LANES), BF16)] * 6
                        + [pltpu.VMEM((S, LANES), F32)] * 6
                        + [pltpu.VMEM((2, DILATED_GROUP * WINDOW_LEN, 2 * WINDOW_LEN), F32)]),
        compiler_params=pltpu.CompilerParams(
            dimension_semantics=("arbitrary", "arbitrary"),
            vmem_limit_bytes=VMEM_LIMIT),
        name="dilated_attention",
    )(q, k, v)


def _ffn_kernel(x_ref, om_ref, od_ref, gm_ref, gd_ref, wo_ref, gf_ref,
                wg_ref, wu_ref, wd_ref, gl_ref, out_ref):
    tile = x_ref.shape[1]
    sub = tile // FFN_ROW_CHUNKS
    lane = lax.broadcasted_iota(jnp.int32, (sub, LANES), 1)
    low_head = lane < HEAD_DIM

    def heads(ref, rows):
        pairs = []
        for p in range(ref.shape[1] // 2):
            a = ref[0, 2 * p, rows, :].astype(F32)
            b = ref[0, 2 * p + 1, rows, :].astype(F32)
            pairs.append(jnp.where(low_head, a, pltpu.roll(b, HEAD_DIM, 1)))
        return jnp.concatenate(pairs, axis=1)

    x1_parts, h_parts = [], []
    for r in range(FFN_ROW_CHUNKS):
        rows = slice(r * sub, (r + 1) * sub)
        mixed = jnp.concatenate([_rms(heads(om_ref, rows), gm_ref[...]),
                                 _rms(heads(od_ref, rows), gd_ref[...])], axis=1).astype(BF16)
        x1 = x_ref[0, rows, :] + jnp.dot(mixed, wo_ref[...], preferred_element_type=F32)
        x1_parts.append(x1)
        h_parts.append(_rms(x1, gf_ref[...]).astype(BF16))
    x2 = jnp.concatenate(x1_parts, axis=0)
    h = jnp.concatenate(h_parts, axis=0)
    for lo, hi in FFN_COL_CHUNKS[:-1]:
        g = jnp.dot(h, wg_ref[:, lo:hi], preferred_element_type=F32)
        u = jnp.dot(h, wu_ref[:, lo:hi], preferred_element_type=F32)
        act = (g * jax.nn.sigmoid(g) * u).astype(BF16)
        x2 = x2 + jnp.dot(act, wd_ref[lo:hi, :], preferred_element_type=F32)
    lo, hi = FFN_COL_CHUNKS[-1]
    g = jnp.dot(h, wg_ref[:, lo:hi], preferred_element_type=F32)
    u = jnp.dot(h, wu_ref[:, lo:hi], preferred_element_type=F32)
    act = (g * jax.nn.sigmoid(g) * u).astype(BF16)
    piece = tile // FFN_TAIL_PIECES
    for r in range(FFN_TAIL_PIECES):
        rows = slice(r * piece, (r + 1) * piece)
        y = x2[rows, :] + jnp.dot(act[rows, :], wd_ref[lo:hi, :], preferred_element_type=F32)
        out_ref[0, rows, :] = _rms(y, gl_ref[...])


def _ffn_call(x, o_moba, o_dil, moba_out_norm, dil_out_norm, w_out, ffn_norm,
              w_gate, w_up, w_down, final_norm):
    B, S, D = x.shape
    tile = TOKEN_TILE
    half = N_HEADS_MOBA * HEAD_DIM

    def const(shape):
        return pl.BlockSpec(shape, lambda b, s: (0,) * len(shape), pipeline_mode=pl.Buffered(1))

    row_spec = pl.BlockSpec((1, tile, D), lambda b, s: (b, s, 0))
    head_spec = pl.BlockSpec((1, N_HEADS_MOBA, tile, LANES), lambda b, s: (b, 0, s, 0))
    return pl.pallas_call(
        _ffn_kernel,
        grid=(B, S // tile),
        in_specs=[row_spec, head_spec, head_spec,
                  const((1, half)), const((1, half)), const((D, D)), const((1, D)),
                  const((D, D_FF)), const((D, D_FF)), const((D_FF, D)), const((1, D))],
        out_specs=row_spec,
        out_shape=jax.ShapeDtypeStruct((B, S, D), F32),
        compiler_params=pltpu.CompilerParams(
            dimension_semantics=("arbitrary", "arbitrary"),
            vmem_limit_bytes=VMEM_LIMIT),
        name="outproj_ffn",
    )(x, o_moba, o_dil, moba_out_norm.reshape(1, half), dil_out_norm.reshape(1, half),
      w_out, ffn_norm.reshape(1, D), w_gate, w_up, w_down, final_norm.reshape(1, D))


def _rope_tables(S):
    inv_freq = ROPE_THETA ** (-np.arange(0, HEAD_DIM, 2, dtype=np.float64) / HEAD_DIM)
    ang = np.arange(S, dtype=np.float64)[:, None] * inv_freq[None, :]
    cos = np.concatenate([np.cos(ang)] * 4, axis=-1)
    sin = np.concatenate([-np.sin(ang), np.sin(ang)] * 2, axis=-1)
    return jnp.asarray(cos, F32), jnp.asarray(sin, F32)


def kernel(x, attn_norm, w_in, moba_out_norm, dil_out_norm, w_out, ffn_norm, w_gate, w_up,
           w_down, final_norm):
    B, S, D = x.shape
    assert DILATIONS == (1, 4, 16)
    assert D == D_MODEL and S % TOKEN_TILE == 0 and S % QKV_TILE == 0
    assert S % (DILATIONS[-1] * WINDOW_LEN) == 0
    assert S // MOBA_BLOCK <= LANES - ONEHOT_LANE0
    cos, sin = _rope_tables(S)
    v0 = 2 * D_MODEL
    wv_moba = lax.optimization_barrier(w_in[:, v0:v0 + N_HEADS_MOBA * HEAD_DIM])
    wv_moba_t = wv_moba.T.astype(BF16)
    qm, km, vmt, qd, kd, vd = _qkv_call(x, attn_norm, w_in.astype(BF16), wv_moba_t, cos, sin)
    o_moba = _moba_call(qm, km, vmt)
    o_dil = _dilated_call(qd, kd, vd)
    return _ffn_call(x, o_moba, o_dil, moba_out_norm, dil_out_norm, w_out.astype(BF16),
                     ffn_norm, w_gate.astype(BF16), w_up.astype(BF16), w_down.astype(BF16),
                     final_norm)
```
